```python
import jax, jax.numpy as jnp
from jax import lax
import numpy as np

D_MODEL = 1024
BATCH = 8
SEQ = 2048
DEPTH = 2
DEC_BATCH = 128
DEC_SEQ = 8
PAST_LEN = 2048
PAGE_SIZE = 128

N_META = 16
SSM_WIDTH = D_MODEL // 2
SSM_GROUP = 16
SSM_GROUPS = SSM_WIDTH // SSM_GROUP
SSM_STATE = 64
HEAD_DIM = 64
N_HEADS = (D_MODEL // 2) // HEAD_DIM
ATTN_WIDTH = N_HEADS * HEAD_DIM
Q_BLOCK = 128
D_FF = 2816
CONV_W = 3
LN_EPS = 1e-5
DEEPNORM_ALPHA = (2.0 * DEPTH) ** 0.25
DEEPNORM_BETA = (8.0 * DEPTH) ** -0.25
DT_MIN = 1e-3
DT_MAX = 1e-1
SB_BIAS_LO = -7.0
SB_BIAS_HI = -5.0
O_SSM = SSM_WIDTH
O_QKV = O_SSM + 3 * ATTN_WIDTH
O_GA = O_QKV + D_MODEL
IN_COLS = O_GA + D_MODEL

kernel_name = "hybrid_s5_stickbreak_convffn_step"


def layer_norm(x, g, b):
    xf = x.astype(jnp.float32)
    mu = jnp.mean(xf, axis=-1, keepdims=True)
    var = jnp.mean(jnp.square(xf - mu), axis=-1, keepdims=True)
    y = (xf - mu) * lax.rsqrt(var + LN_EPS) * g.astype(jnp.float32) + b.astype(jnp.float32)
    return y.astype(x.dtype)


def s5_scan(u, h0_re, h0_im, a_re, a_im, log_dt, b_re, b_im, c_re, c_im, d_skip):
    f32 = jnp.float32
    bsz, t_len, _ = u.shape
    ug = u.astype(f32).reshape(bsz, t_len, SSM_GROUPS, SSM_GROUP)
    dt = jnp.exp(log_dt.astype(f32))[:, None]
    lr, li = a_re.astype(f32), a_im.astype(f32)
    mag = jnp.exp(lr * dt)
    abar_re, abar_im = mag * jnp.cos(li * dt), mag * jnp.sin(li * dt)
    nr, ni = abar_re - 1.0, abar_im
    den = lr * lr + li * li
    s_re = (nr * lr + ni * li) / den
    s_im = (ni * lr - nr * li) / den
    br, bi = b_re.astype(f32), b_im.astype(f32)
    bb_re = s_re[..., None] * br - s_im[..., None] * bi
    bb_im = s_re[..., None] * bi + s_im[..., None] * br
    x_re = jnp.einsum('btgk,gpk->btgp', ug, bb_re)
    x_im = jnp.einsum('btgk,gpk->btgp', ug, bb_im)
    h0r, h0i = h0_re.astype(f32), h0_im.astype(f32)
    x_re = x_re.at[:, 0].add(abar_re * h0r - abar_im * h0i)
    x_im = x_im.at[:, 0].add(abar_re * h0i + abar_im * h0r)
    ar = jnp.broadcast_to(abar_re, (1, t_len) + abar_re.shape)
    ai = jnp.broadcast_to(abar_im, (1, t_len) + abar_im.shape)

    def combine(e1, e2):
        ar1, ai1, xr1, xi1 = e1
        ar2, ai2, xr2, xi2 = e2
        return (ar2 * ar1 - ai2 * ai1,
                ar2 * ai1 + ai2 * ar1,
                ar2 * xr1 - ai2 * xi1 + xr2,
                ar2 * xi1 + ai2 * xr1 + xi2)

    _, _, h_re, h_im = lax.associative_scan(combine, (ar, ai, x_re, x_im), axis=1)
    y = (jnp.einsum('btgp,gkp->btgk', h_re, c_re.astype(f32))
         - jnp.einsum('btgp,gkp->btgk', h_im, c_im.astype(f32))
         + ug * d_skip.astype(f32).reshape(SSM_GROUPS, SSM_GROUP))
    return (y.reshape(bsz, t_len, SSM_WIDTH).astype(u.dtype),
            h_re[:, -1].astype(u.dtype), h_im[:, -1].astype(u.dtype))


def stick_breaking(q, k, v, bias, q_pos, k_pos):
    f32 = jnp.float32
    z = (jnp.einsum('bqhd,bkhd->bhqk', q.astype(f32), k.astype(f32)) * (HEAD_DIM ** -0.5)
         + bias.astype(f32)[None, :, None, None])
    mask = k_pos[None, :] < q_pos[:, None]
    log_keep = jnp.where(mask, jax.nn.log_sigmoid(-z), 0.0)
    log_w = jax.nn.log_sigmoid(z) + lax.cumsum(log_keep, axis=3, reverse=True) - log_keep
    w = jnp.where(mask, jnp.exp(log_w), 0.0)
    return jnp.einsum('bhqk,bkhd->bqhd', w, v.astype(f32)).astype(q.dtype)


def prompt_attention(q, k, v, bias):
    bsz, t_len = q.shape[0], q.shape[1]
    pos = jnp.arange(t_len)
    o_meta = stick_breaking(q[:, :N_META], k[:, :N_META], v[:, :N_META], bias, pos[:N_META], pos[:N_META])
    n_blk = (t_len - N_META) // Q_BLOCK
    qb = q[:, N_META:].reshape(bsz, n_blk, Q_BLOCK, N_HEADS, HEAD_DIM).transpose(1, 0, 2, 3, 4)
    pb = pos[N_META:].reshape(n_blk, Q_BLOCK)
    ob = lax.map(lambda a: stick_breaking(a[0], k, v, bias, a[1], pos), (qb, pb))
    ob = ob.transpose(1, 0, 2, 3, 4).reshape(bsz, t_len - N_META, N_HEADS, HEAD_DIM)
    return jnp.concatenate([o_meta, ob], axis=1)


def sample_attention(q, k, v, bias, past_k, past_v):
    t_len = q.shape[1]
    past = past_k.shape[1]
    kk = jnp.concatenate([past_k, k], axis=1)
    vv = jnp.concatenate([past_v, v], axis=1)
    k_pos = jnp.arange(past + t_len)
    q_pos = past + jnp.arange(t_len)
    return stick_breaking(q, kk, vv, bias, q_pos, k_pos)


def mixer_sublayer(x, h0_re, h0_im, attend, w_in, a_re, a_im, log_dt, b_re, b_im, c_re, c_im,
                   d_skip, w_glu, b_glu, w_branch_ssm, w_branch_attn, w_out, sb_bias):
    bsz, t_len, _ = x.shape
    proj = x @ w_in
    u = proj[..., :O_SSM]
    qkv = proj[..., O_SSM:O_QKV].reshape(bsz, t_len, 3, N_HEADS, HEAD_DIM)
    g_ssm = proj[..., O_QKV:O_GA]
    g_attn = proj[..., O_GA:]
    q, k, v = qkv[:, :, 0], qkv[:, :, 1], qkv[:, :, 2]
    y_s, h_re, h_im = s5_scan(u, h0_re, h0_im, a_re, a_im, log_dt, b_re, b_im, c_re, c_im, d_skip)
    z = jax.nn.gelu(y_s, approximate=False)
    z = z * jax.nn.sigmoid(z @ w_glu + b_glu)
    o = attend(q, k, v, sb_bias).reshape(bsz, t_len, ATTN_WIDTH)
    merged = jax.nn.sigmoid(g_ssm) * (z @ w_branch_ssm) + jax.nn.sigmoid(g_attn) * (o @ w_branch_attn)
    return merged @ w_out, k, v, h_re, h_im


def conv_ffn(x, conv_state, w_up, conv_w, conv_b, w_down):
    t_len = x.shape[1]
    up = x @ w_up
    ext = jnp.concatenate([conv_state, up], axis=1)
    conv = conv_b + conv_w[0] * ext[:, 0:t_len]
    for i in range(1, CONV_W):
        conv = conv + conv_w[i] * ext[:, i:i + t_len]
    h = jax.nn.gelu(conv[..., :D_FF], approximate=False) * conv[..., D_FF:]
    return h @ w_down, ext[:, -(CONV_W - 1):]


def decoder_layer(x, h0_re, h0_im, conv_state, attend, p):
    (w_in, a_re, a_im, log_dt, b_re, b_im, c_re, c_im, d_skip, w_glu, b_glu, w_branch_ssm,
     w_branch_attn, w_out, ln1_g, ln1_b, w_up, conv_w, conv_b, w_down, ln2_g, ln2_b, sb_bias) = p
    mix, k, v, h_re, h_im = mixer_sublayer(x, h0_re, h0_im, attend, w_in, a_re, a_im, log_dt, b_re, b_im,
                                           c_re, c_im, d_skip, w_glu, b_glu, w_branch_ssm,
                                           w_branch_attn, w_out, sb_bias)
    x = layer_norm(DEEPNORM_ALPHA * x + mix, ln1_g, ln1_b)
    f, new_conv = conv_ffn(x, conv_state, w_up, conv_w, conv_b, w_down)
    x = layer_norm(DEEPNORM_ALPHA * x + f, ln2_g, ln2_b)
    return x, k, v, h_re, h_im, new_conv


def setup_inputs(seed: int = 0) -> dict:
    key = jax.random.key(seed)
    ks = jax.random.split(key, 40)
    f32 = jnp.float32
    n_pages = PAST_LEN // PAGE_SIZE
    n_used = DEC_BATCH * n_pages
    n_pool = (5 * n_used + 3) // 4
    nrm = lambda k, shape, s: jax.random.normal(k, shape, f32) * s
    page_table = jax.random.permutation(ks[0], n_pool)[:n_used].reshape(DEC_BATCH, n_pages).astype(jnp.int32)
    a_im0 = (np.pi * jnp.arange(SSM_STATE, dtype=f32))[None, None, :]
    return {
        "x_prompt": nrm(ks[1], (BATCH, SEQ, D_MODEL), 1.0),
        "x_sample": nrm(ks[2], (DEC_BATCH, DEC_SEQ, D_MODEL), 1.0),
        "cache_k": nrm(ks[3], (DEPTH, n_pool, PAGE_SIZE, N_HEADS, HEAD_DIM), 1.0),
        "cache_v": nrm(ks[4], (DEPTH, n_pool, PAGE_SIZE, N_HEADS, HEAD_DIM), 1.0),
        "state_ssm_re": nrm(ks[5], (DEPTH, DEC_BATCH, SSM_GROUPS, SSM_STATE), 0.3),
        "state_ssm_im": nrm(ks[6], (DEPTH, DEC_BATCH, SSM_GROUPS, SSM_STATE), 0.3),
        "state_conv": nrm(ks[7], (DEPTH, DEC_BATCH, CONV_W - 1, 2 * D_FF), 1.0),
        "page_table": page_table,
        "meta_tokens": nrm(ks[8], (N_META, D_MODEL), 1.0),
        "ln_in_g": 1.0 + nrm(ks[9], (D_MODEL,), 0.02),
        "ln_in_b": nrm(ks[10], (D_MODEL,), 0.02),
        "w_in": nrm(ks[11], (DEPTH, D_MODEL, IN_COLS), D_MODEL ** -0.5),
        "ssm_a_re": -0.5 + nrm(ks[12], (DEPTH, SSM_GROUPS, SSM_STATE), 0.01),
        "ssm_a_im": a_im0 + nrm(ks[13], (DEPTH, SSM_GROUPS, SSM_STATE), 0.01),
        "ssm_log_dt": jax.random.uniform(ks[14], (DEPTH, SSM_GROUPS), f32, np.log(DT_MIN), np.log(DT_MAX)),
        "ssm_b_re": nrm(ks[15], (DEPTH, SSM_GROUPS, SSM_STATE, SSM_GROUP), (2 * SSM_GROUP) ** -0.5),
        "ssm_b_im": nrm(ks[16], (DEPTH, SSM_GROUPS, SSM_STATE, SSM_GROUP), (2 * SSM_GROUP) ** -0.5),
        "ssm_c_re": nrm(ks[17], (DEPTH, SSM_GROUPS, SSM_GROUP, SSM_STATE), SSM_STATE ** -0.5),
        "ssm_c_im": nrm(ks[18], (DEPTH, SSM_GROUPS, SSM_GROUP, SSM_STATE), SSM_STATE ** -0.5),
        "ssm_d": nrm(ks[19], (DEPTH, SSM_WIDTH), 1.0),
        "w_glu": nrm(ks[20], (DEPTH, SSM_WIDTH, SSM_WIDTH), SSM_WIDTH ** -0.5),
        "b_glu": nrm(ks[21], (DEPTH, SSM_WIDTH), 0.02),
        "w_branch_ssm": nrm(ks[22], (DEPTH, SSM_WIDTH, D_MODEL), SSM_WIDTH ** -0.5 * DEEPNORM_BETA),
        "w_branch_attn": nrm(ks[23], (DEPTH, ATTN_WIDTH, D_MODEL), ATTN_WIDTH ** -0.5 * DEEPNORM_BETA),
        "w_out": nrm(ks[24], (DEPTH, D_MODEL, D_MODEL), D_MODEL ** -0.5 * DEEPNORM_BETA),
        "ln1_g": 1.0 + nrm(ks[25], (DEPTH, D_MODEL), 0.02),
        "ln1_b": nrm(ks[26], (DEPTH, D_MODEL), 0.02),
        "w_up": nrm(ks[27], (DEPTH, D_MODEL, 2 * D_FF), D_MODEL ** -0.5),
        "conv_w": nrm(ks[28], (DEPTH, CONV_W, 2 * D_FF), CONV_W ** -0.5),
        "conv_b": nrm(ks[29], (DEPTH, 2 * D_FF), 0.02),
        "w_down": nrm(ks[30], (DEPTH, D_FF, D_MODEL), D_FF ** -0.5 * DEEPNORM_BETA),
        "ln2_g": 1.0 + nrm(ks[31], (DEPTH, D_MODEL), 0.02),
        "ln2_b": nrm(ks[32], (DEPTH, D_MODEL), 0.02),
        "sb_bias": jax.random.uniform(ks[33], (DEPTH, N_HEADS), f32, SB_BIAS_LO, SB_BIAS_HI),
    }


def reference(x_prompt, x_sample, cache_k, cache_v, state_ssm_re, state_ssm_im, state_conv, page_table,
              meta_tokens, ln_in_g, ln_in_b, w_in, ssm_a_re, ssm_a_im, ssm_log_dt, ssm_b_re, ssm_b_im,
              ssm_c_re, ssm_c_im, ssm_d, w_glu, b_glu, w_branch_ssm, w_branch_attn, w_out, ln1_g, ln1_b,
              w_up, conv_w, conv_b, w_down, ln2_g, ln2_b, sb_bias):
    bsz = x_prompt.shape[0]
    dec_b = x_sample.shape[0]
    n_pages = page_table.shape[1]
    past_len = n_pages * cache_k.shape[2]

    meta = jnp.broadcast_to(meta_tokens.astype(x_prompt.dtype)[None], (bsz, N_META, D_MODEL))
    xp = layer_norm(jnp.concatenate([meta, x_prompt], axis=1), ln_in_g, ln_in_b)
    xs = layer_norm(x_sample, ln_in_g, ln_in_b)
    zero_h = jnp.zeros((bsz, SSM_GROUPS, SSM_STATE), xp.dtype)
    zero_conv = jnp.zeros((bsz, CONV_W - 1, 2 * D_FF), xp.dtype)

    kp, vp, hrp, hip, cp = [], [], [], [], []
    ksm, vsm, hrs, his, cs = [], [], [], [], []
    for l in range(DEPTH):
        p = (w_in[l], ssm_a_re[l], ssm_a_im[l], ssm_log_dt[l], ssm_b_re[l], ssm_b_im[l], ssm_c_re[l],
             ssm_c_im[l], ssm_d[l], w_glu[l], b_glu[l], w_branch_ssm[l], w_branch_attn[l], w_out[l],
             ln1_g[l], ln1_b[l], w_up[l], conv_w[l], conv_b[l], w_down[l], ln2_g[l], ln2_b[l], sb_bias[l])
        xp, k_l, v_l, hr_l, hi_l, c_l = decoder_layer(xp, zero_h, zero_h, zero_conv, prompt_attention, p)
        kp.append(k_l); vp.append(v_l); hrp.append(hr_l); hip.append(hi_l); cp.append(c_l)

        past_k = cache_k[l][page_table].reshape(dec_b, past_len, N_HEADS, HEAD_DIM)
        past_v = cache_v[l][page_table].reshape(dec_b, past_len, N_HEADS, HEAD_DIM)
        attend_s = lambda q, k, v, bias, pk=past_k, pv=past_v: sample_attention(q, k, v, bias, pk, pv)
        xs, k_s, v_s, hr_s, hi_s, c_s = decoder_layer(xs, state_ssm_re[l], state_ssm_im[l], state_conv[l],
                                                       attend_s, p)
        ksm.append(k_s); vsm.append(v_s); hrs.append(hr_s); his.append(hi_s); cs.append(c_s)

    y_prompt = xp[:, N_META:]
    y_sample = xs
    return (y_prompt, y_sample,
            jnp.stack(kp), jnp.stack(vp), jnp.stack(hrp), jnp.stack(hip), jnp.stack(cp),
            jnp.stack(ksm), jnp.stack(vsm), jnp.stack(hrs), jnp.stack(his), jnp.stack(cs))
```

```python
import functools
import math

import jax
import jax.numpy as jnp
from jax import lax
from jax.experimental import pallas as pl
from jax.experimental.pallas import tpu as pltpu

F32 = jnp.float32
BF16 = jnp.bfloat16

D_MODEL = 1024
N_META = 16
SSM_WIDTH = 512
SSM_GROUP = 16
SSM_GROUPS = 32
SSM_STATE = 64
N_STATE = SSM_GROUPS * SSM_STATE
HEAD_DIM = 64
N_HEADS = 8
ATTN_WIDTH = 512
D_FF = 2816
CONV_W = 3
LN_EPS = 1e-5
O_Q = SSM_WIDTH
O_K = O_Q + ATTN_WIDTH
O_V = O_K + ATTN_WIDTH
O_GS = O_V + ATTN_WIDTH
O_GA = O_GS + D_MODEL
IN_COLS = O_GA + D_MODEL

SUBLANES = 8
LANES = 128
BF16_ROWS = 16
MXU_COLS = 256
Q_BLOCK = 128
K_BLOCK = 128
FF_CHUNK = MXU_COLS
N_FF_CHUNKS = D_FF // FF_CHUNK
MIB = 1024 * 1024
PROMPT_ROWS_CAP = 700
SAMPLE_ROWS_CAP = 256
SCAN_STEPS_CAP = 96


def _vmem(nbytes):
    return pltpu.CompilerParams(vmem_limit_bytes=int(nbytes))


def _const_spec(shape):
    nd = len(shape)
    return pl.BlockSpec(shape, lambda *_: (0,) * nd, pipeline_mode=pl.Buffered(1))


def _layer_norm(x, g, b):
    mu = jnp.mean(x, axis=-1, keepdims=True)
    xc = x - mu
    var = jnp.mean(xc * xc, axis=-1, keepdims=True)
    return xc * lax.rsqrt(var + LN_EPS) * g + b


def _gelu(x):
    return 0.5 * x * (1.0 + lax.erf(x * math.sqrt(0.5)))


def _iota_div(shape, axis, n):
    assert n & (n - 1) == 0, n
    i = lax.broadcasted_iota(jnp.int32, shape, axis)
    return lax.shift_right_logical(i, n.bit_length() - 1), lax.bitwise_and(i, n - 1)


def _dot(a, b):
    return jnp.dot(a, b, preferred_element_type=F32)


def _dot_nt(a, b):
    return lax.dot_general(a, b, (((1,), (1,)), ((), ())), preferred_element_type=F32)


def _in_proj_kernel(x_ref, g_ref, b_ref, w_ref, *out_refs, ln_input):
    if ln_input:
        xn_ref, u_ref, q_ref, k_ref, v_ref, gs_ref, ga_ref = out_refs
    else:
        u_ref, q_ref, k_ref, v_ref, gs_ref, ga_ref = out_refs
    x = x_ref[...]
    if ln_input:
        x = _layer_norm(x, g_ref[...], b_ref[...])
        xn_ref[...] = x
    xb = x.astype(BF16)
    u_ref[...] = _dot(xb, w_ref[:, 0:O_Q])
    q_ref[...] = _dot(xb, w_ref[:, O_Q:O_K]) * (HEAD_DIM ** -0.5)
    k_ref[...] = _dot(xb, w_ref[:, O_K:O_V])
    v_ref[...] = _dot(xb, w_ref[:, O_V:O_GS])
    gs_ref[...] = jax.nn.sigmoid(_dot(xb, w_ref[:, O_GS:O_GA]))
    ga_ref[...] = jax.nn.sigmoid(_dot(xb, w_ref[:, O_GA:IN_COLS]))


def _in_proj(x3, ln_g, ln_b, w_in_bf, *, tm, ln_input):
    G, T, _ = x3.shape
    row = lambda w: pl.BlockSpec((None, tm, w), lambda g, i: (g, i, 0))
    out_shape, out_specs = [], []
    if ln_input:
        out_shape.append(jax.ShapeDtypeStruct((G, T, D_MODEL), F32))
        out_specs.append(row(D_MODEL))
    out_shape.append(jax.ShapeDtypeStruct((T, G * SSM_WIDTH), F32))
    out_specs.append(pl.BlockSpec((tm, SSM_WIDTH), lambda g, i: (i, g)))
    for _ in range(3):
        out_shape.append(jax.ShapeDtypeStruct((G, T, ATTN_WIDTH), F32))
        out_specs.append(row(ATTN_WIDTH))
    for _ in range(2):
        out_shape.append(jax.ShapeDtypeStruct((G, T, D_MODEL), F32))
        out_specs.append(row(D_MODEL))
    return pl.pallas_call(
        functools.partial(_in_proj_kernel, ln_input=ln_input),
        grid=(G, T // tm),
        in_specs=[row(D_MODEL), _const_spec((1, D_MODEL)), _const_spec((1, D_MODEL)),
                  _const_spec((D_MODEL, IN_COLS))],
        out_specs=out_specs,
        out_shape=out_shape,
        compiler_params=_vmem(56 * MIB),
        name="in_proj",
    )(x3, ln_g, ln_b, w_in_bf)


SCAN_COLS = 512


def _s5_kernel(u_ref, h0r_ref, h0i_ref, ar_ref, ai_ref, wb_ref, wc_ref, d_ref, wglu_ref, bglu_ref,
               z_ref, hr_out, hi_out, xr_ref, xi_ref, hr_s, hi_s, *, nb, tc):
    @pl.when(pl.program_id(0) == 0)
    def _():
        hr_s[...] = h0r_ref[...]
        hi_s[...] = h0i_ref[...]

    half_in = SSM_WIDTH // 2
    half_st = N_STATE // 2
    u = u_ref[...]
    ub = u.astype(BF16)
    for h in range(2):
        ubh = ub[:, h * half_in:(h + 1) * half_in]
        xr_ref[:, h * half_st:(h + 1) * half_st] = _dot(ubh, wb_ref[h, :, :half_st])
        xi_ref[:, h * half_st:(h + 1) * half_st] = _dot(ubh, wb_ref[h, :, half_st:])

    for c in range(N_STATE // SCAN_COLS):
        cols = slice(c * SCAN_COLS, (c + 1) * SCAN_COLS)
        ar = jnp.broadcast_to(ar_ref[:, cols], (SUBLANES, SCAN_COLS))
        ai = jnp.broadcast_to(ai_ref[:, cols], (SUBLANES, SCAN_COLS))

        def seq_group(gi, _):
            r0 = pl.multiple_of(gi * SUBLANES, SUBLANES)

            def step(t, carry):
                hr, hi = carry
                r = pl.multiple_of(t * nb + r0, SUBLANES)
                nhr = ar * hr - ai * hi + xr_ref[pl.ds(r, SUBLANES), cols]
                nhi = ar * hi + ai * hr + xi_ref[pl.ds(r, SUBLANES), cols]
                xr_ref[pl.ds(r, SUBLANES), cols] = nhr
                xi_ref[pl.ds(r, SUBLANES), cols] = nhi
                return nhr, nhi

            hr, hi = lax.fori_loop(0, tc, step,
                                   (hr_s[pl.ds(r0, SUBLANES), cols], hi_s[pl.ds(r0, SUBLANES), cols]),
                                   unroll=2)
            hr_s[pl.ds(r0, SUBLANES), cols] = hr
            hi_s[pl.ds(r0, SUBLANES), cols] = hi
            return 0

        lax.fori_loop(0, nb // SUBLANES, seq_group, 0)

    ys = []
    for h in range(2):
        st = slice(h * half_st, (h + 1) * half_st)
        y = (_dot(xr_ref[:, st].astype(BF16), wc_ref[h, :half_st, :])
             + _dot(xi_ref[:, st].astype(BF16), wc_ref[h, half_st:, :]))
        ch = slice(h * half_in, (h + 1) * half_in)
        ys.append(y + u[:, ch] * d_ref[:, ch])
    y = jnp.concatenate(ys, axis=1)
    g = _gelu(y)
    z_ref[...] = g * jax.nn.sigmoid(_dot(g.astype(BF16), wglu_ref[...]) + bglu_ref[...])
    hr_out[...] = hr_s[...]
    hi_out[...] = hi_s[...]


def _s5(u_tm, h0_re, h0_im, prm, *, nb, tc):
    rows_total = u_tm.shape[0]
    rows = tc * nb
    steps = rows_total // rows
    blk = pl.BlockSpec((rows, SSM_WIDTH), lambda i: (i, 0))
    st = _const_spec((nb, N_STATE))
    return pl.pallas_call(
        functools.partial(_s5_kernel, nb=nb, tc=tc),
        grid=(steps,),
        in_specs=[blk, st, st, _const_spec((1, N_STATE)), _const_spec((1, N_STATE)),
                  _const_spec((2, SSM_WIDTH // 2, N_STATE)), _const_spec((2, N_STATE, SSM_WIDTH // 2)),
                  _const_spec((1, SSM_WIDTH)), _const_spec((SSM_WIDTH, SSM_WIDTH)),
                  _const_spec((1, SSM_WIDTH))],
        out_specs=[blk, pl.BlockSpec((nb, N_STATE), lambda i: (0, 0)),
                   pl.BlockSpec((nb, N_STATE), lambda i: (0, 0))],
        out_shape=[jax.ShapeDtypeStruct((rows_total, SSM_WIDTH), F32),
                   jax.ShapeDtypeStruct((nb, N_STATE), F32),
                   jax.ShapeDtypeStruct((nb, N_STATE), F32)],
        scratch_shapes=[pltpu.VMEM((rows, N_STATE), F32), pltpu.VMEM((rows, N_STATE), F32),
                        pltpu.VMEM((nb, N_STATE), F32), pltpu.VMEM((nb, N_STATE), F32)],
        compiler_params=pltpu.CompilerParams(vmem_limit_bytes=56 * MIB,
                                             dimension_semantics=("arbitrary",)),
        name="s5_scan",
    )(u_tm, h0_re, h0_im, prm["abar_re"], prm["abar_im"], prm["wb"], prm["wc"], prm["d"],
      prm["w_glu"], prm["b_glu"])


def _s5_params(a_re, a_im, log_dt, b_re, b_im, c_re, c_im, d_skip, w_glu, b_glu):
    dt = jnp.exp(log_dt)[:, None]
    mag = jnp.exp(a_re * dt)
    abar_re, abar_im = mag * jnp.cos(a_im * dt), mag * jnp.sin(a_im * dt)
    nr, ni = abar_re - 1.0, abar_im
    den = a_re * a_re + a_im * a_im
    s_re = (nr * a_re + ni * a_im) / den
    s_im = (ni * a_re - nr * a_im) / den
    bb_re = s_re[..., None] * b_re - s_im[..., None] * b_im
    bb_im = s_re[..., None] * b_im + s_im[..., None] * b_re
    eye = jnp.eye(SSM_GROUPS, dtype=F32)
    wb_re = jnp.einsum('gpk,gh->gkhp', bb_re, eye).reshape(SSM_WIDTH, N_STATE)
    wb_im = jnp.einsum('gpk,gh->gkhp', bb_im, eye).reshape(SSM_WIDTH, N_STATE)
    wc_re = jnp.einsum('gkp,gh->gphk', c_re, eye).reshape(N_STATE, SSM_WIDTH)
    wc_im = jnp.einsum('gkp,gh->gphk', c_im, eye).reshape(N_STATE, SSM_WIDTH)
    hi_, hs = SSM_WIDTH // 2, N_STATE // 2
    wb = jnp.stack([jnp.concatenate([wb_re[h * hi_:(h + 1) * hi_, h * hs:(h + 1) * hs],
                                     wb_im[h * hi_:(h + 1) * hi_, h * hs:(h + 1) * hs]], axis=1)
                    for h in range(2)])
    wc = jnp.stack([jnp.concatenate([wc_re[h * hs:(h + 1) * hs, h * hi_:(h + 1) * hi_],
                                     -wc_im[h * hs:(h + 1) * hs, h * hi_:(h + 1) * hi_]], axis=0)
                    for h in range(2)])
    return {"abar_re": abar_re.reshape(1, N_STATE), "abar_im": abar_im.reshape(1, N_STATE),
            "wb": wb.astype(BF16), "wc": wc.astype(BF16), "d": d_skip.reshape(1, SSM_WIDTH),
            "w_glu": w_glu.astype(BF16), "b_glu": b_glu.reshape(1, SSM_WIDTH)}


def _later_keys_matrix(n):
    m = lax.broadcasted_iota(jnp.int32, (n, n), 0)
    j = lax.broadcasted_iota(jnp.int32, (n, n), 1)
    return jnp.where(m > j, 1.0, 0.0).astype(BF16)


def _stick_block(s, bias, mask, carry, later):
    z = s + bias
    soft = jnp.log1p(jnp.exp(-jnp.abs(z)))
    log_beta = jnp.minimum(z, 0.0) - soft
    log_keep = log_beta - z
    if mask is not None:
        log_keep = jnp.where(mask, log_keep, 0.0)
    hi = log_keep.astype(BF16)
    lo = (log_keep - hi.astype(F32)).astype(BF16)
    after = _dot(hi, later) + _dot(lo, later)
    w = jnp.exp(log_beta + after + carry)
    if mask is not None:
        w = jnp.where(mask, w, 0.0)
    return w, carry + after[:, 0:1] + log_keep[:, 0:1]


def _attn_prompt_kernel(bias_ref, q_ref, k_ref, v_ref, o_ref, kb_ref, vb_ref, *, n_real_blocks):
    kb_ref[...] = k_ref[...].astype(BF16)
    vb_ref[...] = v_ref[...].astype(BF16)
    later_full = _later_keys_matrix(K_BLOCK)
    lane_head, _ = _iota_div((1, LANES), 1, HEAD_DIM)

    def q_block(r0, tq, n_full):
        qi = lax.broadcasted_iota(jnp.int32, (tq, tq), 0)
        kj = lax.broadcasted_iota(jnp.int32, (tq, tq), 1)
        diag_mask = kj < qi
        diag_later = later_full if tq == K_BLOCK else _later_keys_matrix(tq)
        for c in range(ATTN_WIDTH // LANES):
            lanes = slice(c * LANES, (c + 1) * LANES)
            qc = q_ref[pl.ds(r0, tq), lanes]
            qm = [jnp.where(lane_head == e, qc, 0.0).astype(BF16) for e in range(2)]
            bias = [bias_ref[2 * c + e] for e in range(2)]

            def keys(k0, tk, mask, later, state):
                kc = kb_ref[pl.ds(k0, tk), lanes]
                vc = vb_ref[pl.ds(k0, tk), lanes]
                new = []
                for e in range(2):
                    carry, acc = state[e]
                    w, carry = _stick_block(_dot_nt(qm[e], kc), bias[e], mask, carry, later)
                    new.append((carry, acc + _dot(w.astype(BF16), vc)))
                return tuple(new)

            zero = (jnp.zeros((tq, 1), F32), jnp.zeros((tq, LANES), F32))
            state = keys(r0, tq, diag_mask, diag_later, (zero, zero))
            if n_full is not None:
                def full(jj, st):
                    k0 = pl.multiple_of(r0 - (jj + 1) * K_BLOCK, BF16_ROWS)
                    return keys(k0, K_BLOCK, None, later_full, st)
                state = lax.fori_loop(0, n_full, full, state)
                meta_mask = lax.broadcasted_iota(jnp.int32, (tq, K_BLOCK), 1) < N_META
                state = keys(0, K_BLOCK, meta_mask, later_full, state)
            o_ref[pl.ds(r0, tq), lanes] = jnp.where(lane_head == 0, state[0][1], state[1][1])

    q_block(0, N_META, None)

    def real(i, _):
        q_block(pl.multiple_of(N_META + i * Q_BLOCK, BF16_ROWS), Q_BLOCK, i)
        return 0

    lax.fori_loop(0, n_real_blocks, real, 0)


def _attn_prompt(q, k, v, bias):
    B, T, _ = q.shape
    blk = pl.BlockSpec((None, T, ATTN_WIDTH), lambda b: (b, 0, 0))
    return pl.pallas_call(
        functools.partial(_attn_prompt_kernel, n_real_blocks=(T - N_META) // Q_BLOCK),
        grid=(B,),
        in_specs=[pl.BlockSpec(memory_space=pltpu.SMEM), blk, blk, blk],
        out_specs=blk,
        out_shape=jax.ShapeDtypeStruct((B, T, ATTN_WIDTH), F32),
        scratch_shapes=[pltpu.VMEM((T, ATTN_WIDTH), BF16), pltpu.VMEM((T, ATTN_WIDTH), BF16)],
        compiler_params=_vmem(56 * MIB),
        name="attn_prompt",
    )(bias, q, k, v)


PAGES_PER_STEP = 4


def _attn_sample_kernel(pt_ref, bias_ref, q_ref, kn_ref, vn_ref, *rest, t_new, page, n_steps):
    k_refs = rest[:PAGES_PER_STEP]
    v_refs = rest[PAGES_PER_STEP:2 * PAGES_PER_STEP]
    o_ref, qm_ref, acc_ref, carry_ref, pad_k, pad_v = rest[2 * PAGES_PER_STEP:]
    rows = N_HEADS * t_new
    p = pl.program_id(1)
    later = _later_keys_matrix(page)
    row_head, _ = _iota_div((rows, ATTN_WIDTH), 0, t_new)
    lane_head, _ = _iota_div((rows, ATTN_WIDTH), 1, HEAD_DIM)
    own = row_head == lane_head
    bias = bias_ref[...]

    def keys(kf, vf, mask):
        s = _dot_nt(qm_ref[...], kf.astype(BF16))
        w, carry = _stick_block(s, bias, mask, carry_ref[...], later)
        carry_ref[...] = carry
        acc_ref[...] += _dot(w.astype(BF16), vf.astype(BF16))

    @pl.when(p == 0)
    def _():
        q = jnp.concatenate([q_ref[...]] * N_HEADS, axis=0)
        qm_ref[...] = jnp.where(own, q, 0.0).astype(BF16)
        acc_ref[...] = jnp.zeros_like(acc_ref)
        carry_ref[...] = jnp.zeros_like(carry_ref)
        pad_k[...] = jnp.zeros_like(pad_k)
        pad_v[...] = jnp.zeros_like(pad_v)
        pad_k[0:t_new, :] = kn_ref[...]
        pad_v[0:t_new, :] = vn_ref[...]
        _, tok = _iota_div((rows, page), 0, t_new)
        key = lax.broadcasted_iota(jnp.int32, (rows, page), 1)
        keys(pad_k[...], pad_v[...], key < tok)

    for a in range(PAGES_PER_STEP):
        keys(k_refs[a][...], v_refs[a][...], None)

    @pl.when(p == n_steps - 1)
    def _():
        acc = jnp.where(own, acc_ref[...], 0.0)
        out = acc[0:t_new]
        for h in range(1, N_HEADS):
            out = out + acc[h * t_new:(h + 1) * t_new]
        o_ref[...] = out


def _attn_sample(q, k_new, v_new, cache_k, cache_v, page_table, bias_rows, pool_base):
    S, t_new, _ = q.shape
    n_pages = page_table.shape[1]
    page = cache_k.shape[1]
    n_steps = n_pages // PAGES_PER_STEP
    rows = N_HEADS * t_new
    tok = pl.BlockSpec((None, t_new, ATTN_WIDTH), lambda s, p, pt: (s, 0, 0))

    def page_spec(a):
        return pl.BlockSpec(
            (None, page, ATTN_WIDTH),
            lambda s, p, pt: (pool_base + pt[s, n_pages - 1 - p * PAGES_PER_STEP - a], 0, 0))

    grid_spec = pltpu.PrefetchScalarGridSpec(
        num_scalar_prefetch=1,
        grid=(S, n_steps),
        in_specs=[pl.BlockSpec((rows, 1), lambda s, p, pt: (0, 0)), tok, tok, tok]
        + [page_spec(a) for a in range(PAGES_PER_STEP)] * 2,
        out_specs=tok,
        scratch_shapes=[pltpu.VMEM((rows, ATTN_WIDTH), BF16), pltpu.VMEM((rows, ATTN_WIDTH), F32),
                        pltpu.VMEM((rows, 1), F32), pltpu.VMEM((page, ATTN_WIDTH), F32),
                        pltpu.VMEM((page, ATTN_WIDTH), F32)],
    )
    return pl.pallas_call(
        functools.partial(_attn_sample_kernel, t_new=t_new, page=page, n_steps=n_steps),
        grid_spec=grid_spec,
        out_shape=jax.ShapeDtypeStruct((S, t_new, ATTN_WIDTH), F32),
        compiler_params=pltpu.CompilerParams(vmem_limit_bytes=32 * MIB,
                                             dimension_semantics=("arbitrary", "arbitrary")),
        name="attn_sample",
    )(page_table, bias_rows, q, k_new, v_new, *([cache_k] * PAGES_PER_STEP), *([cache_v] * PAGES_PER_STEP))


def _mix_kernel(x_ref, gs_ref, ga_ref, z_ref, o_ref, wbs_ref, wba_ref, wout_ref, g_ref, b_ref, y_ref, *, alpha):
    merged = (gs_ref[...] * _dot(z_ref[...].astype(BF16), wbs_ref[...])
              + ga_ref[...] * _dot(o_ref[...].astype(BF16), wba_ref[...]))
    mix = _dot(merged.astype(BF16), wout_ref[...])
    y_ref[...] = _layer_norm(alpha * x_ref[...] + mix, g_ref[...], b_ref[...])


def _mix(x3, gs, ga, z_tm, o, wbs, wba, wout, ln_g, ln_b, *, tm, alpha):
    G, T, _ = x3.shape
    row = lambda w: pl.BlockSpec((None, tm, w), lambda g, i: (g, i, 0))
    return pl.pallas_call(
        functools.partial(_mix_kernel, alpha=alpha),
        grid=(G, T // tm),
        in_specs=[row(D_MODEL), row(D_MODEL), row(D_MODEL),
                  pl.BlockSpec((tm, SSM_WIDTH), lambda g, i: (i, g)), row(ATTN_WIDTH),
                  _const_spec((SSM_WIDTH, D_MODEL)), _const_spec((ATTN_WIDTH, D_MODEL)),
                  _const_spec((D_MODEL, D_MODEL)), _const_spec((1, D_MODEL)), _const_spec((1, D_MODEL))],
        out_specs=row(D_MODEL),
        out_shape=jax.ShapeDtypeStruct((G, T, D_MODEL), F32),
        compiler_params=_vmem(48 * MIB),
        name="mix",
    )(x3, gs, ga, z_tm, o, wbs, wba, wout, ln_g, ln_b)


def _chunk_major(a):
    *lead, rows, cols = a.shape
    return jnp.swapaxes(a.reshape(*lead, rows, cols // FF_CHUNK, FF_CHUNK), -3, -2)


def _from_chunk_major(a):
    *lead, n, rows, _ = a.shape
    return jnp.swapaxes(a, -3, -2).reshape(*lead, rows, n * FF_CHUNK)


def _ffn_kernel(x_ref, prev_ref, wup_ref, cw_ref, cb_ref, wdown_ref, g_ref, b_ref,
                y_ref, up_ref, acc_ref, *, alpha, seq_len, carry_rows):
    tm = x_ref.shape[0]
    x = x_ref[...]
    xb = x.astype(BF16)
    if seq_len == tm:
        t_loc = lax.broadcasted_iota(jnp.int32, (tm, 1), 0)
    else:
        _, t_loc = _iota_div((tm, 1), 0, seq_len)
    if carry_rows:
        @pl.when(pl.program_id(1) == 0)
        def _():
            up_ref[...] = prev_ref[...]

    def conv(j):
        up = _dot(xb, wup_ref[j])
        if carry_rows:
            before2 = jnp.broadcast_to(up_ref[j, SUBLANES - 2:SUBLANES - 1, :], (tm, FF_CHUNK))
            before1 = jnp.broadcast_to(up_ref[j, SUBLANES - 1:SUBLANES, :], (tm, FF_CHUNK))
            up1 = jnp.where(t_loc >= 1, pltpu.roll(up, 1, axis=0), before1)
            up2 = jnp.where(t_loc >= 2, pltpu.roll(up, 2, axis=0), jnp.where(t_loc == 0, before2, before1))
            up_ref[j] = up[tm - SUBLANES:tm, :]
        else:
            prev = prev_ref[j]
            up1 = jnp.where(t_loc >= 1, pltpu.roll(up, 1, axis=0), pltpu.roll(prev, tm - 1, axis=0))
            up2 = jnp.where(t_loc >= 2, pltpu.roll(up, 2, axis=0), prev)
            up_ref[j] = up
        cw = cw_ref[j]
        return cb_ref[j] + cw[0:1] * up2 + cw[1:2] * up1 + cw[2:3] * up

    acc_ref[...] = jnp.zeros_like(acc_ref)

    def chunk(c, _):
        h = _gelu(conv(c)) * conv(c + N_FF_CHUNKS)
        acc_ref[...] += _dot(h.astype(BF16), wdown_ref[c])
        return 0

    lax.fori_loop(0, N_FF_CHUNKS, chunk, 0)
    y_ref[...] = _layer_norm(alpha * x + acc_ref[...], g_ref[...], b_ref[...])


def _ffn(x3, prev, wup, conv_w, conv_b, wdown, ln_g, ln_b, *, tm, alpha, seq_len, carry_rows):
    G, T, _ = x3.shape
    n2 = 2 * N_FF_CHUNKS
    row = lambda w: pl.BlockSpec((None, tm, w), lambda g, i: (g, i, 0))
    if carry_rows:
        up_spec = pl.BlockSpec((None, n2, SUBLANES, FF_CHUNK), lambda g, i: (g, 0, 0, 0))
        up_shape = jax.ShapeDtypeStruct((G, n2, SUBLANES, FF_CHUNK), F32)
    else:
        up_spec = pl.BlockSpec((None, n2, tm, FF_CHUNK), lambda g, i: (g, 0, i, 0))
        up_shape = jax.ShapeDtypeStruct((G, n2, T, FF_CHUNK), F32)
    return pl.pallas_call(
        functools.partial(_ffn_kernel, alpha=alpha, seq_len=seq_len, carry_rows=carry_rows),
        grid=(G, T // tm),
        in_specs=[row(D_MODEL), up_spec, _const_spec((n2, D_MODEL, FF_CHUNK)),
                  _const_spec((n2, CONV_W, FF_CHUNK)), _const_spec((n2, 1, FF_CHUNK)),
                  _const_spec((N_FF_CHUNKS, FF_CHUNK, D_MODEL)), _const_spec((1, D_MODEL)),
                  _const_spec((1, D_MODEL))],
        out_specs=[row(D_MODEL), up_spec],
        out_shape=[jax.ShapeDtypeStruct((G, T, D_MODEL), F32), up_shape],
        scratch_shapes=[pltpu.VMEM((tm, D_MODEL), F32)],
        compiler_params=pltpu.CompilerParams(vmem_limit_bytes=56 * MIB,
                                             dimension_semantics=("arbitrary", "arbitrary")),
        name="conv_ffn",
    )(x3, prev, wup, conv_w, conv_b, wdown, ln_g, ln_b)


def _largest_divisor(n, cap, multiple):
    best = None
    for d in range(multiple, cap + 1, multiple):
        if n % d == 0:
            best = d
    assert best is not None, (n, cap, multiple)
    return best


def _layer_weights(l, w_in, ssm, w_branch_ssm, w_branch_attn, w_out, ln1_g, ln1_b, w_up, conv_w, conv_b,
                   w_down, ln2_g, ln2_b, sb_bias):
    row = lambda a: a[l].reshape(1, -1)
    return {
        "w_in": w_in[l].astype(BF16), "s5": _s5_params(*[a[l] for a in ssm]),
        "wbs": w_branch_ssm[l].astype(BF16), "wba": w_branch_attn[l].astype(BF16),
        "wout": w_out[l].astype(BF16), "ln1_g": row(ln1_g), "ln1_b": row(ln1_b),
        "wup": _chunk_major(w_up[l].astype(BF16)), "conv_w": _chunk_major(conv_w[l]),
        "conv_b": _chunk_major(row(conv_b)),
        "wdown": w_down[l].astype(BF16).reshape(N_FF_CHUNKS, FF_CHUNK, D_MODEL),
        "ln2_g": row(ln2_g), "ln2_b": row(ln2_b), "sb_bias": sb_bias[l],
    }


def kernel(x_prompt, x_sample, cache_k, cache_v, state_ssm_re, state_ssm_im, state_conv, page_table, meta_tokens, ln_in_g, ln_in_b, w_in, ssm_a_re, ssm_a_im, ssm_log_dt, ssm_b_re, ssm_b_im, ssm_c_re, ssm_c_im, ssm_d, w_glu, b_glu, w_branch_ssm, w_branch_attn, w_out, ln1_g, ln1_b, w_up, conv_w, conv_b, w_down, ln2_g, ln2_b, sb_bias):
    B, seq, _ = x_prompt.shape
    S, t_new, _ = x_sample.shape
    depth = w_in.shape[0]
    alpha = (2.0 * depth) ** 0.25
    T = N_META + seq
    assert B % SUBLANES == 0 and S % SUBLANES == 0 and seq % Q_BLOCK == 0 and t_new >= CONV_W - 1
    n_pool, page = cache_k.shape[1], cache_k.shape[2]
    assert page == K_BLOCK and page_table.shape[1] % PAGES_PER_STEP == 0
    ck = cache_k.reshape(depth * n_pool, page, ATTN_WIDTH)
    cv = cache_v.reshape(depth * n_pool, page, ATTN_WIDTH)

    ssm = (ssm_a_re, ssm_a_im, ssm_log_dt, ssm_b_re, ssm_b_im, ssm_c_re, ssm_c_im, ssm_d, w_glu, b_glu)
    lng, lnb = ln_in_g.reshape(1, -1), ln_in_b.reshape(1, -1)

    tm_p = _largest_divisor(T, PROMPT_ROWS_CAP, BF16_ROWS)
    tc_p = _largest_divisor(T, SCAN_STEPS_CAP, 2)
    rows_s = S * t_new
    tm_s = _largest_divisor(rows_s, SAMPLE_ROWS_CAP, BF16_ROWS * t_new)

    meta = jnp.broadcast_to(meta_tokens[None], (B, N_META, D_MODEL))
    xp = jnp.concatenate([meta, x_prompt], axis=1)
    xs = x_sample.reshape(1, rows_s, D_MODEL)
    zeros_state = jnp.zeros((B, N_STATE), F32)
    zeros_conv = jnp.zeros((B, 2 * N_FF_CHUNKS, SUBLANES, FF_CHUNK), F32)
    to_tm = lambda a: a.reshape(S, t_new, -1).transpose(1, 0, 2).reshape(rows_s, -1)
    from_tm = lambda a: a.reshape(t_new, S, -1).transpose(1, 0, 2).reshape(rows_s, -1)

    outs_p, outs_s = [], []
    for l in range(depth):
        w = _layer_weights(l, w_in, ssm, w_branch_ssm, w_branch_attn, w_out, ln1_g, ln1_b, w_up, conv_w,
                           conv_b, w_down, ln2_g, ln2_b, sb_bias)
        first = l == 0

        res = _in_proj(xp, lng, lnb, w["w_in"], tm=tm_p, ln_input=first)
        if first:
            xp, *res = res
        u, q, k, v, gs, ga = res
        z, hr, hi = _s5(u.reshape(T * B, SSM_WIDTH), zeros_state, zeros_state, w["s5"], nb=B, tc=tc_p)
        o = _attn_prompt(q, k, v, w["sb_bias"])
        x1 = _mix(xp, gs, ga, z.reshape(T, B * SSM_WIDTH), o, w["wbs"], w["wba"], w["wout"],
                  w["ln1_g"], w["ln1_b"], tm=tm_p, alpha=alpha)
        xp, tail = _ffn(x1, zeros_conv, w["wup"], w["conv_w"], w["conv_b"], w["wdown"], w["ln2_g"],
                        w["ln2_b"], tm=tm_p, alpha=alpha, seq_len=tm_p, carry_rows=True)
        outs_p.append((k.reshape(B, T, N_HEADS, HEAD_DIM), v.reshape(B, T, N_HEADS, HEAD_DIM),
                       hr.reshape(B, SSM_GROUPS, SSM_STATE), hi.reshape(B, SSM_GROUPS, SSM_STATE),
                       _from_chunk_major(tail)[:, SUBLANES - (CONV_W - 1):]))

        res = _in_proj(xs, lng, lnb, w["w_in"], tm=tm_s, ln_input=first)
        if first:
            xs, *res = res
        u, q, k, v, gs, ga = res
        z, hr, hi = _s5(to_tm(u), state_ssm_re[l].reshape(S, N_STATE), state_ssm_im[l].reshape(S, N_STATE),
                        w["s5"], nb=S, tc=t_new)
        k3, v3 = k.reshape(S, t_new, ATTN_WIDTH), v.reshape(S, t_new, ATTN_WIDTH)
        bias_rows = jnp.repeat(w["sb_bias"], t_new).reshape(N_HEADS * t_new, 1)
        o = _attn_sample(q.reshape(S, t_new, ATTN_WIDTH), k3, v3, ck, cv, page_table, bias_rows, l * n_pool)
        x1 = _mix(xs, gs, ga, from_tm(z), o.reshape(1, rows_s, ATTN_WIDTH), w["wbs"], w["wba"], w["wout"],
                  w["ln1_g"], w["ln1_b"], tm=tm_s, alpha=alpha)
        prev = jnp.pad(state_conv[l], ((0, 0), (0, t_new - (CONV_W - 1)), (0, 0))).reshape(1, rows_s, 2 * D_FF)
        xs, up = _ffn(x1, _chunk_major(prev), w["wup"], w["conv_w"], w["conv_b"], w["wdown"], w["ln2_g"],
                      w["ln2_b"], tm=tm_s, alpha=alpha, seq_len=t_new, carry_rows=False)
        up = up.reshape(2 * N_FF_CHUNKS, S, t_new, FF_CHUNK)[:, :, t_new - (CONV_W - 1):]
        outs_s.append((k3.reshape(S, t_new, N_HEADS, HEAD_DIM), v3.reshape(S, t_new, N_HEADS, HEAD_DIM),
                       hr.reshape(S, SSM_GROUPS, SSM_STATE), hi.reshape(S, SSM_GROUPS, SSM_STATE),
                       up.transpose(1, 2, 0, 3).reshape(S, CONV_W - 1, 2 * D_FF)))

    stack = lambda outs, j: jnp.stack([o[j] for o in outs])
    return (xp[:, N_META:], xs.reshape(S, t_new, D_MODEL),
            *[stack(outs_p, j) for j in range(5)], *[stack(outs_s, j) for j in range(5)])
```

```python
import functools
import math

import jax
import jax.numpy as jnp
from jax import lax
from jax.experimental import pallas as pl
from jax.experimental.pallas import tpu as pltpu

F32 = jnp.float32
BF16 = jnp.bfloat16

D_MODEL = 1024
N_META = 16
SSM_WIDTH = 512
SSM_GROUP = 16
SSM_GROUPS = 32
SSM_STATE = 64
N_STATE = SSM_GROUPS * SSM_STATE
HEAD_DIM = 64
N_HEADS = 8
ATTN_WIDTH = 512
D_FF = 2816
CONV_W = 3
LN_EPS = 1e-5
O_Q = SSM_WIDTH
O_K = O_Q + ATTN_WIDTH
O_V = O_K + ATTN_WIDTH
O_GS = O_V + ATTN_WIDTH
O_GA = O_GS + D_MODEL
IN_COLS = O_GA + D_MODEL

SUBLANES = 8
LANES = 128
BF16_ROWS = 16
MXU_COLS = 256
Q_BLOCK = 256
K_BLOCK = 256
FF_CHUNK = MXU_COLS
N_FF_CHUNKS = D_FF // FF_CHUNK
MIB = 1024 * 1024
PROMPT_ROWS_CAP = 700
SAMPLE_ROWS_CAP = 256
SCAN_STEPS_CAP = 96


def _vmem(nbytes):
    return pltpu.CompilerParams(vmem_limit_bytes=int(nbytes))


def _const_spec(shape):
    nd = len(shape)
    return pl.BlockSpec(shape, lambda *_: (0,) * nd, pipeline_mode=pl.Buffered(1))


def _layer_norm(x, g, b):
    mu = jnp.mean(x, axis=-1, keepdims=True)
    xc = x - mu
    var = jnp.mean(xc * xc, axis=-1, keepdims=True)
    return xc * lax.rsqrt(var + LN_EPS) * g + b


def _gelu(x):
    return 0.5 * x * (1.0 + lax.erf(x * math.sqrt(0.5)))


def _iota_div(shape, axis, n):
    assert n & (n - 1) == 0, n
    i = lax.broadcasted_iota(jnp.int32, shape, axis)
    return lax.shift_right_logical(i, n.bit_length() - 1), lax.bitwise_and(i, n - 1)


def _dot(a, b):
    return jnp.dot(a, b, preferred_element_type=F32)


def _dot_nt(a, b):
    return lax.dot_general(a, b, (((1,), (1,)), ((), ())), preferred_element_type=F32)


def _in_proj_kernel(x_ref, g_ref, b_ref, w_ref, *out_refs, ln_input):
    if ln_input:
        xn_ref, u_ref, q_ref, k_ref, v_ref, gs_ref, ga_ref = out_refs
    else:
        u_ref, q_ref, k_ref, v_ref, gs_ref, ga_ref = out_refs
    x = x_ref[...]
    if ln_input:
        x = _layer_norm(x, g_ref[...], b_ref[...])
        xn_ref[...] = x
    xb = x.astype(BF16)
    u_ref[...] = _dot(xb, w_ref[:, 0:O_Q])
    q_ref[...] = _dot(xb, w_ref[:, O_Q:O_K]) * (HEAD_DIM ** -0.5)
    k_ref[...] = _dot(xb, w_ref[:, O_K:O_V])
    v_ref[...] = _dot(xb, w_ref[:, O_V:O_GS])
    gs_ref[...] = jax.nn.sigmoid(_dot(xb, w_ref[:, O_GS:O_GA]))
    ga_ref[...] = jax.nn.sigmoid(_dot(xb, w_ref[:, O_GA:IN_COLS]))


def _in_proj(x3, ln_g, ln_b, w_in_bf, *, tm, ln_input):
    G, T, _ = x3.shape
    row = lambda w: pl.BlockSpec((None, tm, w), lambda g, i: (g, i, 0))
    out_shape, out_specs = [], []
    if ln_input:
        out_shape.append(jax.ShapeDtypeStruct((G, T, D_MODEL), F32))
        out_specs.append(row(D_MODEL))
    out_shape.append(jax.ShapeDtypeStruct((T, G * SSM_WIDTH), F32))
    out_specs.append(pl.BlockSpec((tm, SSM_WIDTH), lambda g, i: (i, g)))
    for _ in range(3):
        out_shape.append(jax.ShapeDtypeStruct((G, T, ATTN_WIDTH), F32))
        out_specs.append(row(ATTN_WIDTH))
    for _ in range(2):
        out_shape.append(jax.ShapeDtypeStruct((G, T, D_MODEL), F32))
        out_specs.append(row(D_MODEL))
    return pl.pallas_call(
        functools.partial(_in_proj_kernel, ln_input=ln_input),
        grid=(G, T // tm),
        in_specs=[row(D_MODEL), _const_spec((1, D_MODEL)), _const_spec((1, D_MODEL)),
                  _const_spec((D_MODEL, IN_COLS))],
        out_specs=out_specs,
        out_shape=out_shape,
        compiler_params=_vmem(56 * MIB),
        name="in_proj",
    )(x3, ln_g, ln_b, w_in_bf)


SCAN_COLS = 512


def _s5_kernel(u_ref, h0r_ref, h0i_ref, ar_ref, ai_ref, wb_ref, wc_ref, d_ref, wglu_ref, bglu_ref,
               z_ref, hr_out, hi_out, xr_ref, xi_ref, hr_s, hi_s, *, nb, tc):
    @pl.when(pl.program_id(0) == 0)
    def _():
        hr_s[...] = h0r_ref[...]
        hi_s[...] = h0i_ref[...]

    half_in = SSM_WIDTH // 2
    half_st = N_STATE // 2
    u = u_ref[...]
    ub = u.astype(BF16)
    for h in range(2):
        ubh = ub[:, h * half_in:(h + 1) * half_in]
        xr_ref[:, h * half_st:(h + 1) * half_st] = _dot(ubh, wb_ref[h, :, :half_st])
        xi_ref[:, h * half_st:(h + 1) * half_st] = _dot(ubh, wb_ref[h, :, half_st:])

    for c in range(N_STATE // SCAN_COLS):
        cols = slice(c * SCAN_COLS, (c + 1) * SCAN_COLS)
        ar = jnp.broadcast_to(ar_ref[:, cols], (SUBLANES, SCAN_COLS))
        ai = jnp.broadcast_to(ai_ref[:, cols], (SUBLANES, SCAN_COLS))

        def seq_group(gi, _):
            r0 = pl.multiple_of(gi * SUBLANES, SUBLANES)

            def step(t, carry):
                hr, hi = carry
                r = pl.multiple_of(t * nb + r0, SUBLANES)
                nhr = ar * hr - ai * hi + xr_ref[pl.ds(r, SUBLANES), cols]
                nhi = ar * hi + ai * hr + xi_ref[pl.ds(r, SUBLANES), cols]
                xr_ref[pl.ds(r, SUBLANES), cols] = nhr
                xi_ref[pl.ds(r, SUBLANES), cols] = nhi
                return nhr, nhi

            hr, hi = lax.fori_loop(0, tc, step,
                                   (hr_s[pl.ds(r0, SUBLANES), cols], hi_s[pl.ds(r0, SUBLANES), cols]),
                                   unroll=2)
            hr_s[pl.ds(r0, SUBLANES), cols] = hr
            hi_s[pl.ds(r0, SUBLANES), cols] = hi
            return 0

        lax.fori_loop(0, nb // SUBLANES, seq_group, 0)

    ys = []
    for h in range(2):
        st = slice(h * half_st, (h + 1) * half_st)
        y = (_dot(xr_ref[:, st].astype(BF16), wc_ref[h, :half_st, :])
             + _dot(xi_ref[:, st].astype(BF16), wc_ref[h, half_st:, :]))
        ch = slice(h * half_in, (h + 1) * half_in)
        ys.append(y + u[:, ch] * d_ref[:, ch])
    y = jnp.concatenate(ys, axis=1)
    g = _gelu(y)
    z_ref[...] = g * jax.nn.sigmoid(_dot(g.astype(BF16), wglu_ref[...]) + bglu_ref[...])
    hr_out[...] = hr_s[...]
    hi_out[...] = hi_s[...]


def _s5(u_tm, h0_re, h0_im, prm, *, nb, tc):
    rows_total = u_tm.shape[0]
    rows = tc * nb
    steps = rows_total // rows
    blk = pl.BlockSpec((rows, SSM_WIDTH), lambda i: (i, 0))
    st = _const_spec((nb, N_STATE))
    return pl.pallas_call(
        functools.partial(_s5_kernel, nb=nb, tc=tc),
        grid=(steps,),
        in_specs=[blk, st, st, _const_spec((1, N_STATE)), _const_spec((1, N_STATE)),
                  _const_spec((2, SSM_WIDTH // 2, N_STATE)), _const_spec((2, N_STATE, SSM_WIDTH // 2)),
                  _const_spec((1, SSM_WIDTH)), _const_spec((SSM_WIDTH, SSM_WIDTH)),
                  _const_spec((1, SSM_WIDTH))],
        out_specs=[blk, pl.BlockSpec((nb, N_STATE), lambda i: (0, 0)),
                   pl.BlockSpec((nb, N_STATE), lambda i: (0, 0))],
        out_shape=[jax.ShapeDtypeStruct((rows_total, SSM_WIDTH), F32),
                   jax.ShapeDtypeStruct((nb, N_STATE), F32),
                   jax.ShapeDtypeStruct((nb, N_STATE), F32)],
        scratch_shapes=[pltpu.VMEM((rows, N_STATE), F32), pltpu.VMEM((rows, N_STATE), F32),
                        pltpu.VMEM((nb, N_STATE), F32), pltpu.VMEM((nb, N_STATE), F32)],
        compiler_params=pltpu.CompilerParams(vmem_limit_bytes=56 * MIB,
                                             dimension_semantics=("arbitrary",)),
        name="s5_scan",
    )(u_tm, h0_re, h0_im, prm["abar_re"], prm["abar_im"], prm["wb"], prm["wc"], prm["d"],
      prm["w_glu"], prm["b_glu"])


def _s5_params(a_re, a_im, log_dt, b_re, b_im, c_re, c_im, d_skip, w_glu, b_glu):
    dt = jnp.exp(log_dt)[:, None]
    mag = jnp.exp(a_re * dt)
    abar_re, abar_im = mag * jnp.cos(a_im * dt), mag * jnp.sin(a_im * dt)
    nr, ni = abar_re - 1.0, abar_im
    den = a_re * a_re + a_im * a_im
    s_re = (nr * a_re + ni * a_im) / den
    s_im = (ni * a_re - nr * a_im) / den
    bb_re = s_re[..., None] * b_re - s_im[..., None] * b_im
    bb_im = s_re[..., None] * b_im + s_im[..., None] * b_re
    eye = jnp.eye(SSM_GROUPS, dtype=F32)
    wb_re = jnp.einsum('gpk,gh->gkhp', bb_re, eye).reshape(SSM_WIDTH, N_STATE)
    wb_im = jnp.einsum('gpk,gh->gkhp', bb_im, eye).reshape(SSM_WIDTH, N_STATE)
    wc_re = jnp.einsum('gkp,gh->gphk', c_re, eye).reshape(N_STATE, SSM_WIDTH)
    wc_im = jnp.einsum('gkp,gh->gphk', c_im, eye).reshape(N_STATE, SSM_WIDTH)
    hi_, hs = SSM_WIDTH // 2, N_STATE // 2
    wb = jnp.stack([jnp.concatenate([wb_re[h * hi_:(h + 1) * hi_, h * hs:(h + 1) * hs],
                                     wb_im[h * hi_:(h + 1) * hi_, h * hs:(h + 1) * hs]], axis=1)
                    for h in range(2)])
    wc = jnp.stack([jnp.concatenate([wc_re[h * hs:(h + 1) * hs, h * hi_:(h + 1) * hi_],
                                     -wc_im[h * hs:(h + 1) * hs, h * hi_:(h + 1) * hi_]], axis=0)
                    for h in range(2)])
    return {"abar_re": abar_re.reshape(1, N_STATE), "abar_im": abar_im.reshape(1, N_STATE),
            "wb": wb.astype(BF16), "wc": wc.astype(BF16), "d": d_skip.reshape(1, SSM_WIDTH),
            "w_glu": w_glu.astype(BF16), "b_glu": b_glu.reshape(1, SSM_WIDTH)}


def _from_key_on_matrix(n):
    m = lax.broadcasted_iota(jnp.int32, (n, n), 0)
    j = lax.broadcasted_iota(jnp.int32, (n, n), 1)
    return jnp.where(m >= j, 1.0, 0.0).astype(BF16)


def _softplus(z, mask):
    p = jnp.maximum(z, 0.0) + jnp.log(1.0 + jnp.exp(-jnp.abs(z)))
    if mask is not None:
        p = jnp.where(mask, p, 0.0)
    return p.astype(BF16)


def _stick_weights(z, sums, carry, mask):
    w = jnp.exp(z - (sums + carry))
    if mask is not None:
        w = jnp.where(mask, w, 0.0)
    return w.astype(BF16)


def _attn_prompt_kernel(bias_ref, q_ref, k_ref, v_ref, o_ref, kb_ref, vb_ref, from_ref, qm_ref, z_ref, p_ref,
                        w_ref, car_ref, acc_ref, *, seq_len):
    t_pad = kb_ref.shape[0]
    for src, dst in ((k_ref, kb_ref), (v_ref, vb_ref)):
        dst[0:seq_len, :] = src[...].astype(BF16)
        dst[seq_len:t_pad, :] = jnp.zeros((t_pad - seq_len, ATTN_WIDTH), BF16)
    from_ref[...] = _from_key_on_matrix(K_BLOCK)
    lane_head, _ = _iota_div((1, LANES), 1, HEAD_DIM)
    n_chunks = ATTN_WIDTH // LANES
    chunk = lambda c: slice(c * LANES, (c + 1) * LANES)
    bias = [bias_ref[h] for h in range(N_HEADS)]

    def q_block(r0, tq, k_diag, n_full):
        qi = lax.broadcasted_iota(jnp.int32, (tq, K_BLOCK), 0) + (r0 - k_diag)
        kj = lax.broadcasted_iota(jnp.int32, (tq, K_BLOCK), 1)
        diag_mask = kj < qi
        rows = lambda h: slice(h * Q_BLOCK, h * Q_BLOCK + tq)
        for c in range(n_chunks):
            qc = q_ref[pl.ds(r0, tq), chunk(c)]
            for e in range(2):
                qm_ref[2 * c + e, 0:tq, :] = jnp.where(lane_head == e, qc, 0.0).astype(BF16)

        def sweep(k0, mask, first):
            for h in range(N_HEADS):
                z = _dot_nt(qm_ref[h, 0:tq, :], kb_ref[pl.ds(k0, K_BLOCK), chunk(h // 2)]) + bias[h]
                z_ref[h, 0:tq, :] = z
                p_ref[rows(h), :] = _softplus(z, mask)
            if tq == Q_BLOCK:
                sums = _dot(p_ref[...], from_ref[...])
            for h in range(N_HEADS):
                sums_h = sums[rows(h)] if tq == Q_BLOCK else _dot(p_ref[rows(h), :], from_ref[...])
                carry = jnp.zeros((tq, 1), F32) if first else car_ref[h, 0:tq, :]
                w_ref[h, 0:tq, :] = _stick_weights(z_ref[h, 0:tq, :], sums_h, carry, mask)
                car_ref[h, 0:tq, :] = carry + sums_h[:, 0:1]
            for h in range(N_HEADS):
                pv = _dot(w_ref[h, 0:tq, :], vb_ref[pl.ds(k0, K_BLOCK), chunk(h // 2)])
                acc_ref[h, 0:tq, :] = pv if first else acc_ref[h, 0:tq, :] + pv

        sweep(k_diag, diag_mask, True)

        def full(jj, _):
            sweep(pl.multiple_of(k_diag - (jj + 1) * K_BLOCK, K_BLOCK), None, False)
            return 0

        lax.fori_loop(0, n_full, full, 0)
        for c in range(n_chunks):
            o_ref[pl.ds(r0, tq), chunk(c)] = jnp.where(
                lane_head == 0, acc_ref[2 * c, 0:tq, :], acc_ref[2 * c + 1, 0:tq, :])

    ratio = K_BLOCK // Q_BLOCK
    n_q = seq_len // Q_BLOCK

    def whole(i, _):
        jd = lax.shift_right_logical(i, ratio.bit_length() - 1)
        q_block(pl.multiple_of(i * Q_BLOCK, Q_BLOCK), Q_BLOCK, pl.multiple_of(jd * K_BLOCK, K_BLOCK), jd)
        return 0

    lax.fori_loop(0, n_q, whole, 0)
    if seq_len > n_q * Q_BLOCK:
        r0 = n_q * Q_BLOCK
        q_block(r0, seq_len - r0, r0 // K_BLOCK * K_BLOCK, r0 // K_BLOCK)


def _attn_prompt(q, k, v, bias):
    B, T, _ = q.shape
    assert K_BLOCK % Q_BLOCK == 0 and (K_BLOCK // Q_BLOCK) & (K_BLOCK // Q_BLOCK - 1) == 0
    assert (T % Q_BLOCK) % BF16_ROWS == 0
    t_pad = -(-T // K_BLOCK) * K_BLOCK
    blk = pl.BlockSpec((None, T, ATTN_WIDTH), lambda b: (b, 0, 0))
    return pl.pallas_call(
        functools.partial(_attn_prompt_kernel, seq_len=T),
        grid=(B,),
        in_specs=[pl.BlockSpec(memory_space=pltpu.SMEM), blk, blk, blk],
        out_specs=blk,
        out_shape=jax.ShapeDtypeStruct((B, T, ATTN_WIDTH), F32),
        scratch_shapes=[pltpu.VMEM((t_pad, ATTN_WIDTH), BF16), pltpu.VMEM((t_pad, ATTN_WIDTH), BF16),
                        pltpu.VMEM((K_BLOCK, K_BLOCK), BF16),
                        pltpu.VMEM((N_HEADS, Q_BLOCK, LANES), BF16),
                        pltpu.VMEM((N_HEADS, Q_BLOCK, K_BLOCK), F32),
                        pltpu.VMEM((N_HEADS * Q_BLOCK, K_BLOCK), BF16),
                        pltpu.VMEM((N_HEADS, Q_BLOCK, K_BLOCK), BF16),
                        pltpu.VMEM((N_HEADS, Q_BLOCK, 1), F32),
                        pltpu.VMEM((N_HEADS, Q_BLOCK, LANES), F32)],
        compiler_params=_vmem(56 * MIB),
        name="attn_prompt",
    )(bias, q, k, v)


def _attn_sample_kernel(pt_ref, bias_ref, q_ref, kn_ref, vn_ref, *rest, t_new, page, n_pages):
    k_refs, v_refs = rest[:n_pages], rest[n_pages:2 * n_pages]
    o_ref, pad_k, pad_v = rest[2 * n_pages:]
    rows = N_HEADS * t_new
    from_on = _from_key_on_matrix(page)
    bias = bias_ref[...]
    head_lanes = lambda h: slice(h * HEAD_DIM, (h + 1) * HEAD_DIM)
    qh = [q_ref[:, head_lanes(h)].astype(BF16) for h in range(N_HEADS)]

    def head_rows(ref):
        return lambda h: ref[pl.ds(h, page, stride=N_HEADS), :].astype(BF16)

    for pad, new in ((pad_k, kn_ref), (pad_v, vn_ref)):
        pad[...] = jnp.zeros_like(pad)
        for h in range(N_HEADS):
            pad[h, 0:t_new, :] = new[:, head_lanes(h)]
    _, tok = _iota_div((rows, page), 0, t_new)
    new_mask = lax.broadcasted_iota(jnp.int32, (rows, page), 1) < tok
    blocks = [(lambda h: pad_k[h].astype(BF16), lambda h: pad_v[h].astype(BF16), new_mask)]
    blocks += [(head_rows(k_refs[a]), head_rows(v_refs[a]), None) for a in range(n_pages)]

    scored = []
    for k_of, _, mask in blocks:
        z = jnp.concatenate([_dot_nt(qh[h], k_of(h)) for h in range(N_HEADS)], axis=0) + bias
        scored.append((z, _dot(_softplus(z, mask), from_on)))
    carry = jnp.zeros((rows, 1), F32)
    acc = [jnp.zeros((t_new, HEAD_DIM), F32)] * N_HEADS
    for (_, v_of, mask), (z, sums) in zip(blocks, scored):
        w = _stick_weights(z, sums, carry, mask)
        carry = carry + sums[:, 0:1]
        acc = [acc[h] + _dot(w[h * t_new:(h + 1) * t_new], v_of(h)) for h in range(N_HEADS)]
    o_ref[...] = jnp.concatenate(acc, axis=1)


def _attn_sample(q, k_new, v_new, cache_k, cache_v, page_table, bias_rows, layer):
    S, t_new, _ = q.shape
    n_pages = page_table.shape[1]
    page = cache_k.shape[2] // N_HEADS
    rows = N_HEADS * t_new
    assert t_new % BF16_ROWS == 0 or t_new == SUBLANES
    tok = pl.BlockSpec((None, t_new, ATTN_WIDTH), lambda s, pt: (s, 0, 0))

    def page_spec(a):
        return pl.BlockSpec((None, None, page * N_HEADS, HEAD_DIM),
                            lambda s, pt: (layer, pt[s, n_pages - 1 - a], 0, 0))

    page_bytes = page * N_HEADS * LANES * 4
    grid_spec = pltpu.PrefetchScalarGridSpec(
        num_scalar_prefetch=1,
        grid=(S,),
        in_specs=[pl.BlockSpec((rows, 1), lambda s, pt: (0, 0)), tok, tok, tok]
        + [page_spec(a) for a in range(n_pages)] * 2,
        out_specs=tok,
        scratch_shapes=[pltpu.VMEM((N_HEADS, page, HEAD_DIM), F32), pltpu.VMEM((N_HEADS, page, HEAD_DIM), F32)],
    )
    return pl.pallas_call(
        functools.partial(_attn_sample_kernel, t_new=t_new, page=page, n_pages=n_pages),
        grid_spec=grid_spec,
        out_shape=jax.ShapeDtypeStruct((S, t_new, ATTN_WIDTH), F32),
        compiler_params=pltpu.CompilerParams(vmem_limit_bytes=4 * n_pages * page_bytes + 16 * MIB,
                                             dimension_semantics=("arbitrary",)),
        name="attn_sample",
    )(page_table, bias_rows, q, k_new, v_new, *([cache_k] * n_pages), *([cache_v] * n_pages))


def _mix_kernel(x_ref, gs_ref, ga_ref, z_ref, o_ref, wbs_ref, wba_ref, wout_ref, g_ref, b_ref, y_ref, *, alpha):
    merged = (gs_ref[...] * _dot(z_ref[...].astype(BF16), wbs_ref[...])
              + ga_ref[...] * _dot(o_ref[...].astype(BF16), wba_ref[...]))
    mix = _dot(merged.astype(BF16), wout_ref[...])
    y_ref[...] = _layer_norm(alpha * x_ref[...] + mix, g_ref[...], b_ref[...])


def _mix(x3, gs, ga, z_tm, o, wbs, wba, wout, ln_g, ln_b, *, tm, alpha):
    G, T, _ = x3.shape
    row = lambda w: pl.BlockSpec((None, tm, w), lambda g, i: (g, i, 0))
    return pl.pallas_call(
        functools.partial(_mix_kernel, alpha=alpha),
        grid=(G, T // tm),
        in_specs=[row(D_MODEL), row(D_MODEL), row(D_MODEL),
                  pl.BlockSpec((tm, SSM_WIDTH), lambda g, i: (i, g)), row(ATTN_WIDTH),
                  _const_spec((SSM_WIDTH, D_MODEL)), _const_spec((ATTN_WIDTH, D_MODEL)),
                  _const_spec((D_MODEL, D_MODEL)), _const_spec((1, D_MODEL)), _const_spec((1, D_MODEL))],
        out_specs=row(D_MODEL),
        out_shape=jax.ShapeDtypeStruct((G, T, D_MODEL), F32),
        compiler_params=_vmem(48 * MIB),
        name="mix",
    )(x3, gs, ga, z_tm, o, wbs, wba, wout, ln_g, ln_b)


def _chunk_major(a):
    *lead, rows, cols = a.shape
    return jnp.swapaxes(a.reshape(*lead, rows, cols // FF_CHUNK, FF_CHUNK), -3, -2)


def _from_chunk_major(a):
    *lead, n, rows, _ = a.shape
    return jnp.swapaxes(a, -3, -2).reshape(*lead, rows, n * FF_CHUNK)


def _ffn_kernel(x_ref, prev_ref, wup_ref, cw_ref, cb_ref, wdown_ref, g_ref, b_ref,
                y_ref, up_ref, acc_ref, *, alpha, seq_len, carry_rows):
    tm = x_ref.shape[0]
    x = x_ref[...]
    xb = x.astype(BF16)
    if seq_len == tm:
        t_loc = lax.broadcasted_iota(jnp.int32, (tm, 1), 0)
    else:
        _, t_loc = _iota_div((tm, 1), 0, seq_len)
    if carry_rows:
        @pl.when(pl.program_id(1) == 0)
        def _():
            up_ref[...] = prev_ref[...]

    def conv(j):
        up = _dot(xb, wup_ref[j])
        if carry_rows:
            before2 = jnp.broadcast_to(up_ref[j, SUBLANES - 2:SUBLANES - 1, :], (tm, FF_CHUNK))
            before1 = jnp.broadcast_to(up_ref[j, SUBLANES - 1:SUBLANES, :], (tm, FF_CHUNK))
            up1 = jnp.where(t_loc >= 1, pltpu.roll(up, 1, axis=0), before1)
            up2 = jnp.where(t_loc >= 2, pltpu.roll(up, 2, axis=0), jnp.where(t_loc == 0, before2, before1))
            up_ref[j] = up[tm - SUBLANES:tm, :]
        else:
            prev = prev_ref[j]
            up1 = jnp.where(t_loc >= 1, pltpu.roll(up, 1, axis=0), pltpu.roll(prev, tm - 1, axis=0))
            up2 = jnp.where(t_loc >= 2, pltpu.roll(up, 2, axis=0), prev)
            up_ref[j] = up
        cw = cw_ref[j]
        return cb_ref[j] + cw[0:1] * up2 + cw[1:2] * up1 + cw[2:3] * up

    acc_ref[...] = jnp.zeros_like(acc_ref)

    def chunk(c, _):
        h = _gelu(conv(c)) * conv(c + N_FF_CHUNKS)
        acc_ref[...] += _dot(h.astype(BF16), wdown_ref[c])
        return 0

    lax.fori_loop(0, N_FF_CHUNKS, chunk, 0)
    y_ref[...] = _layer_norm(alpha * x + acc_ref[...], g_ref[...], b_ref[...])


def _ffn(x3, prev, wup, conv_w, conv_b, wdown, ln_g, ln_b, *, tm, alpha, seq_len, carry_rows):
    G, T, _ = x3.shape
    n2 = 2 * N_FF_CHUNKS
    row = lambda w: pl.BlockSpec((None, tm, w), lambda g, i: (g, i, 0))
    if carry_rows:
        up_spec = pl.BlockSpec((None, n2, SUBLANES, FF_CHUNK), lambda g, i: (g, 0, 0, 0))
        up_shape = jax.ShapeDtypeStruct((G, n2, SUBLANES, FF_CHUNK), F32)
    else:
        up_spec = pl.BlockSpec((None, n2, tm, FF_CHUNK), lambda g, i: (g, 0, i, 0))
        up_shape = jax.ShapeDtypeStruct((G, n2, T, FF_CHUNK), F32)
    return pl.pallas_call(
        functools.partial(_ffn_kernel, alpha=alpha, seq_len=seq_len, carry_rows=carry_rows),
        grid=(G, T // tm),
        in_specs=[row(D_MODEL), up_spec, _const_spec((n2, D_MODEL, FF_CHUNK)),
                  _const_spec((n2, CONV_W, FF_CHUNK)), _const_spec((n2, 1, FF_CHUNK)),
                  _const_spec((N_FF_CHUNKS, FF_CHUNK, D_MODEL)), _const_spec((1, D_MODEL)),
                  _const_spec((1, D_MODEL))],
        out_specs=[row(D_MODEL), up_spec],
        out_shape=[jax.ShapeDtypeStruct((G, T, D_MODEL), F32), up_shape],
        scratch_shapes=[pltpu.VMEM((tm, D_MODEL), F32)],
        compiler_params=pltpu.CompilerParams(vmem_limit_bytes=56 * MIB,
                                             dimension_semantics=("arbitrary", "arbitrary")),
        name="conv_ffn",
    )(x3, prev, wup, conv_w, conv_b, wdown, ln_g, ln_b)


def _largest_divisor(n, cap, multiple):
    best = None
    for d in range(multiple, cap + 1, multiple):
        if n % d == 0:
            best = d
    assert best is not None, (n, cap, multiple)
    return best


def _layer_weights(l, w_in, ssm, w_branch_ssm, w_branch_attn, w_out, ln1_g, ln1_b, w_up, conv_w, conv_b,
                   w_down, ln2_g, ln2_b, sb_bias):
    row = lambda a: a[l].reshape(1, -1)
    return {
        "w_in": w_in[l].astype(BF16), "s5": _s5_params(*[a[l] for a in ssm]),
        "wbs": w_branch_ssm[l].astype(BF16), "wba": w_branch_attn[l].astype(BF16),
        "wout": w_out[l].astype(BF16), "ln1_g": row(ln1_g), "ln1_b": row(ln1_b),
        "wup": _chunk_major(w_up[l].astype(BF16)), "conv_w": _chunk_major(conv_w[l]),
        "conv_b": _chunk_major(row(conv_b)),
        "wdown": w_down[l].astype(BF16).reshape(N_FF_CHUNKS, FF_CHUNK, D_MODEL),
        "ln2_g": row(ln2_g), "ln2_b": row(ln2_b), "sb_bias": sb_bias[l],
    }


def kernel(x_prompt, x_sample, cache_k, cache_v, state_ssm_re, state_ssm_im, state_conv, page_table, meta_tokens, ln_in_g, ln_in_b, w_in, ssm_a_re, ssm_a_im, ssm_log_dt, ssm_b_re, ssm_b_im, ssm_c_re, ssm_c_im, ssm_d, w_glu, b_glu, w_branch_ssm, w_branch_attn, w_out, ln1_g, ln1_b, w_up, conv_w, conv_b, w_down, ln2_g, ln2_b, sb_bias):
    B, seq, _ = x_prompt.shape
    S, t_new, _ = x_sample.shape
    depth = w_in.shape[0]
    alpha = (2.0 * depth) ** 0.25
    T = N_META + seq
    assert B % SUBLANES == 0 and S % SUBLANES == 0 and seq % Q_BLOCK == 0 and t_new >= CONV_W - 1
    assert cache_k.shape[3:] == (N_HEADS, HEAD_DIM)
    ck = cache_k.reshape(*cache_k.shape[:2], -1, HEAD_DIM)
    cv = cache_v.reshape(*cache_v.shape[:2], -1, HEAD_DIM)

    ssm = (ssm_a_re, ssm_a_im, ssm_log_dt, ssm_b_re, ssm_b_im, ssm_c_re, ssm_c_im, ssm_d, w_glu, b_glu)
    lng, lnb = ln_in_g.reshape(1, -1), ln_in_b.reshape(1, -1)

    tm_p = _largest_divisor(T, PROMPT_ROWS_CAP, BF16_ROWS)
    tc_p = _largest_divisor(T, SCAN_STEPS_CAP, 2)
    rows_s = S * t_new
    tm_s = _largest_divisor(rows_s, SAMPLE_ROWS_CAP, BF16_ROWS * t_new)

    meta = jnp.broadcast_to(meta_tokens[None], (B, N_META, D_MODEL))
    xp = jnp.concatenate([meta, x_prompt], axis=1)
    xs = x_sample.reshape(1, rows_s, D_MODEL)
    zeros_state = jnp.zeros((B, N_STATE), F32)
    zeros_conv = jnp.zeros((B, 2 * N_FF_CHUNKS, SUBLANES, FF_CHUNK), F32)
    to_tm = lambda a: a.reshape(S, t_new, -1).transpose(1, 0, 2).reshape(rows_s, -1)
    from_tm = lambda a: a.reshape(t_new, S, -1).transpose(1, 0, 2).reshape(rows_s, -1)

    outs_p, outs_s = [], []
    for l in range(depth):
        w = _layer_weights(l, w_in, ssm, w_branch_ssm, w_branch_attn, w_out, ln1_g, ln1_b, w_up, conv_w,
                           conv_b, w_down, ln2_g, ln2_b, sb_bias)
        first = l == 0

        res = _in_proj(xp, lng, lnb, w["w_in"], tm=tm_p, ln_input=first)
        if first:
            xp, *res = res
        u, q, k, v, gs, ga = res
        z, hr, hi = _s5(u.reshape(T * B, SSM_WIDTH), zeros_state, zeros_state, w["s5"], nb=B, tc=tc_p)
        o = _attn_prompt(q, k, v, w["sb_bias"])
        x1 = _mix(xp, gs, ga, z.reshape(T, B * SSM_WIDTH), o, w["wbs"], w["wba"], w["wout"],
                  w["ln1_g"], w["ln1_b"], tm=tm_p, alpha=alpha)
        xp, tail = _ffn(x1, zeros_conv, w["wup"], w["conv_w"], w["conv_b"], w["wdown"], w["ln2_g"],
                        w["ln2_b"], tm=tm_p, alpha=alpha, seq_len=tm_p, carry_rows=True)
        outs_p.append((k.reshape(B, T, N_HEADS, HEAD_DIM), v.reshape(B, T, N_HEADS, HEAD_DIM),
                       hr.reshape(B, SSM_GROUPS, SSM_STATE), hi.reshape(B, SSM_GROUPS, SSM_STATE),
                       _from_chunk_major(tail)[:, SUBLANES - (CONV_W - 1):]))

        res = _in_proj(xs, lng, lnb, w["w_in"], tm=tm_s, ln_input=first)
        if first:
            xs, *res = res
        u, q, k, v, gs, ga = res
        z, hr, hi = _s5(to_tm(u), state_ssm_re[l].reshape(S, N_STATE), state_ssm_im[l].reshape(S, N_STATE),
                        w["s5"], nb=S, tc=t_new)
        k3, v3 = k.reshape(S, t_new, ATTN_WIDTH), v.reshape(S, t_new, ATTN_WIDTH)
        bias_rows = jnp.repeat(w["sb_bias"], t_new).reshape(N_HEADS * t_new, 1)
        o = _attn_sample(q.reshape(S, t_new, ATTN_WIDTH), k3, v3, ck, cv, page_table, bias_rows, l)
        x1 = _mix(xs, gs, ga, from_tm(z), o.reshape(1, rows_s, ATTN_WIDTH), w["wbs"], w["wba"], w["wout"],
                  w["ln1_g"], w["ln1_b"], tm=tm_s, alpha=alpha)
        prev = jnp.pad(state_conv[l], ((0, 0), (0, t_new - (CONV_W - 1)), (0, 0))).reshape(1, rows_s, 2 * D_FF)
        xs, up = _ffn(x1, _chunk_major(prev), w["wup"], w["conv_w"], w["conv_b"], w["wdown"], w["ln2_g"],
                      w["ln2_b"], tm=tm_s, alpha=alpha, seq_len=t_new, carry_rows=False)
        up = up.reshape(2 * N_FF_CHUNKS, S, t_new, FF_CHUNK)[:, :, t_new - (CONV_W - 1):]
        outs_s.append((k3.reshape(S, t_new, N_HEADS, HEAD_DIM), v3.reshape(S, t_new, N_HEADS, HEAD_DIM),
                       hr.reshape(S, SSM_GROUPS, SSM_STATE), hi.reshape(S, SSM_GROUPS, SSM_STATE),
                       up.transpose(1, 2, 0, 3).reshape(S, CONV_W - 1, 2 * D_FF)))

    stack = lambda outs, j: jnp.stack([o[j] for o in outs])
    return (xp[:, N_META:], xs.reshape(S, t_new, D_MODEL),
            *[stack(outs_p, j) for j in range(5)], *[stack(outs_s, j) for j in range(5)])
```

```python
import functools
import math

import jax
import jax.numpy as jnp
from jax import lax
from jax.experimental import pallas as pl
from jax.experimental.pallas import tpu as pltpu

F32 = jnp.float32
BF16 = jnp.bfloat16

D_MODEL = 1024
N_META = 16
SSM_WIDTH = 512
SSM_GROUP = 16
SSM_GROUPS = 32
SSM_STATE = 64
N_STATE = SSM_GROUPS * SSM_STATE
HEAD_DIM = 64
N_HEADS = 8
ATTN_WIDTH = 512
D_FF = 2816
CONV_W = 3
LN_EPS = 1e-5
O_Q = SSM_WIDTH
O_K = O_Q + ATTN_WIDTH
O_V = O_K + ATTN_WIDTH
O_GS = O_V + ATTN_WIDTH
O_GA = O_GS + D_MODEL
IN_COLS = O_GA + D_MODEL

SUBLANES = 8
LANES = 128
BF16_ROWS = 16
MXU_COLS = 256
Q_BLOCK = 256
K_BLOCK = 256
FF_CHUNK = MXU_COLS
N_FF_CHUNKS = D_FF // FF_CHUNK
MIB = 1024 * 1024
PROMPT_ROWS_CAP = 700
SAMPLE_ROWS_CAP = 256
SCAN_STEPS_CAP = 96


def _vmem(nbytes):
    return pltpu.CompilerParams(vmem_limit_bytes=int(nbytes))


def _const_spec(shape):
    nd = len(shape)
    return pl.BlockSpec(shape, lambda *_: (0,) * nd, pipeline_mode=pl.Buffered(1))


def _layer_norm(x, g, b):
    mu = jnp.mean(x, axis=-1, keepdims=True)
    xc = x - mu
    var = jnp.mean(xc * xc, axis=-1, keepdims=True)
    return xc * lax.rsqrt(var + LN_EPS) * g + b


def _gelu(x):
    return 0.5 * x * (1.0 + lax.erf(x * math.sqrt(0.5)))


def _iota_div(shape, axis, n):
    assert n & (n - 1) == 0, n
    i = lax.broadcasted_iota(jnp.int32, shape, axis)
    return lax.shift_right_logical(i, n.bit_length() - 1), lax.bitwise_and(i, n - 1)


def _dot(a, b):
    return jnp.dot(a, b, preferred_element_type=F32)


def _dot_nt(a, b):
    return lax.dot_general(a, b, (((1,), (1,)), ((), ())), preferred_element_type=F32)


def _in_proj_kernel(x_ref, g_ref, b_ref, w_ref, *out_refs, ln_input):
    if ln_input:
        xn_ref, u_ref, q_ref, k_ref, v_ref, gs_ref, ga_ref = out_refs
    else:
        u_ref, q_ref, k_ref, v_ref, gs_ref, ga_ref = out_refs
    x = x_ref[...]
    if ln_input:
        x = _layer_norm(x, g_ref[...], b_ref[...])
        xn_ref[...] = x
    xb = x.astype(BF16)
    u_ref[...] = _dot(xb, w_ref[:, 0:O_Q])
    q_ref[...] = _dot(xb, w_ref[:, O_Q:O_K]) * (HEAD_DIM ** -0.5)
    k_ref[...] = _dot(xb, w_ref[:, O_K:O_V])
    v_ref[...] = _dot(xb, w_ref[:, O_V:O_GS])
    gs_ref[...] = jax.nn.sigmoid(_dot(xb, w_ref[:, O_GS:O_GA]))
    ga_ref[...] = jax.nn.sigmoid(_dot(xb, w_ref[:, O_GA:IN_COLS]))


def _in_proj(x3, ln_g, ln_b, w_in_bf, *, tm, ln_input):
    G, T, _ = x3.shape
    row = lambda w: pl.BlockSpec((None, tm, w), lambda g, i: (g, i, 0))
    out_shape, out_specs = [], []
    if ln_input:
        out_shape.append(jax.ShapeDtypeStruct((G, T, D_MODEL), F32))
        out_specs.append(row(D_MODEL))
    out_shape.append(jax.ShapeDtypeStruct((T, G * SSM_WIDTH), F32))
    out_specs.append(pl.BlockSpec((tm, SSM_WIDTH), lambda g, i: (i, g)))
    for _ in range(3):
        out_shape.append(jax.ShapeDtypeStruct((G, T, ATTN_WIDTH), F32))
        out_specs.append(row(ATTN_WIDTH))
    for _ in range(2):
        out_shape.append(jax.ShapeDtypeStruct((G, T, D_MODEL), F32))
        out_specs.append(row(D_MODEL))
    return pl.pallas_call(
        functools.partial(_in_proj_kernel, ln_input=ln_input),
        grid=(G, T // tm),
        in_specs=[row(D_MODEL), _const_spec((1, D_MODEL)), _const_spec((1, D_MODEL)),
                  _const_spec((D_MODEL, IN_COLS))],
        out_specs=out_specs,
        out_shape=out_shape,
        compiler_params=_vmem(56 * MIB),
        name="in_proj",
    )(x3, ln_g, ln_b, w_in_bf)


SCAN_COLS = 512


def _s5_kernel(u_ref, h0r_ref, h0i_ref, ar_ref, ai_ref, wb_ref, wc_ref, d_ref, wglu_ref, bglu_ref,
               z_ref, hr_out, hi_out, xr_ref, xi_ref, hr_s, hi_s, *, nb, tc):
    @pl.when(pl.program_id(0) == 0)
    def _():
        hr_s[...] = h0r_ref[...]
        hi_s[...] = h0i_ref[...]

    half_in = SSM_WIDTH // 2
    half_st = N_STATE // 2
    u = u_ref[...]
    ub = u.astype(BF16)
    for h in range(2):
        ubh = ub[:, h * half_in:(h + 1) * half_in]
        xr_ref[:, h * half_st:(h + 1) * half_st] = _dot(ubh, wb_ref[h, :, :half_st])
        xi_ref[:, h * half_st:(h + 1) * half_st] = _dot(ubh, wb_ref[h, :, half_st:])

    for c in range(N_STATE // SCAN_COLS):
        cols = slice(c * SCAN_COLS, (c + 1) * SCAN_COLS)
        ar = jnp.broadcast_to(ar_ref[:, cols], (SUBLANES, SCAN_COLS))
        ai = jnp.broadcast_to(ai_ref[:, cols], (SUBLANES, SCAN_COLS))

        def seq_group(gi, _):
            r0 = pl.multiple_of(gi * SUBLANES, SUBLANES)

            def step(t, carry):
                hr, hi = carry
                r = pl.multiple_of(t * nb + r0, SUBLANES)
                nhr = ar * hr - ai * hi + xr_ref[pl.ds(r, SUBLANES), cols]
                nhi = ar * hi + ai * hr + xi_ref[pl.ds(r, SUBLANES), cols]
                xr_ref[pl.ds(r, SUBLANES), cols] = nhr
                xi_ref[pl.ds(r, SUBLANES), cols] = nhi
                return nhr, nhi

            hr, hi = lax.fori_loop(0, tc, step,
                                   (hr_s[pl.ds(r0, SUBLANES), cols], hi_s[pl.ds(r0, SUBLANES), cols]),
                                   unroll=2)
            hr_s[pl.ds(r0, SUBLANES), cols] = hr
            hi_s[pl.ds(r0, SUBLANES), cols] = hi
            return 0

        lax.fori_loop(0, nb // SUBLANES, seq_group, 0)

    ys = []
    for h in range(2):
        st = slice(h * half_st, (h + 1) * half_st)
        y = (_dot(xr_ref[:, st].astype(BF16), wc_ref[h, :half_st, :])
             + _dot(xi_ref[:, st].astype(BF16), wc_ref[h, half_st:, :]))
        ch = slice(h * half_in, (h + 1) * half_in)
        ys.append(y + u[:, ch] * d_ref[:, ch])
    y = jnp.concatenate(ys, axis=1)
    g = _gelu(y)
    z_ref[...] = g * jax.nn.sigmoid(_dot(g.astype(BF16), wglu_ref[...]) + bglu_ref[...])
    hr_out[...] = hr_s[...]
    hi_out[...] = hi_s[...]


def _s5(u_tm, h0_re, h0_im, prm, *, nb, tc):
    rows_total = u_tm.shape[0]
    rows = tc * nb
    steps = rows_total // rows
    blk = pl.BlockSpec((rows, SSM_WIDTH), lambda i: (i, 0))
    st = _const_spec((nb, N_STATE))
    return pl.pallas_call(
        functools.partial(_s5_kernel, nb=nb, tc=tc),
        grid=(steps,),
        in_specs=[blk, st, st, _const_spec((1, N_STATE)), _const_spec((1, N_STATE)),
                  _const_spec((2, SSM_WIDTH // 2, N_STATE)), _const_spec((2, N_STATE, SSM_WIDTH // 2)),
                  _const_spec((1, SSM_WIDTH)), _const_spec((SSM_WIDTH, SSM_WIDTH)),
                  _const_spec((1, SSM_WIDTH))],
        out_specs=[blk, pl.BlockSpec((nb, N_STATE), lambda i: (0, 0)),
                   pl.BlockSpec((nb, N_STATE), lambda i: (0, 0))],
        out_shape=[jax.ShapeDtypeStruct((rows_total, SSM_WIDTH), F32),
                   jax.ShapeDtypeStruct((nb, N_STATE), F32),
                   jax.ShapeDtypeStruct((nb, N_STATE), F32)],
        scratch_shapes=[pltpu.VMEM((rows, N_STATE), F32), pltpu.VMEM((rows, N_STATE), F32),
                        pltpu.VMEM((nb, N_STATE), F32), pltpu.VMEM((nb, N_STATE), F32)],
        compiler_params=pltpu.CompilerParams(vmem_limit_bytes=56 * MIB,
                                             dimension_semantics=("arbitrary",)),
        name="s5_scan",
    )(u_tm, h0_re, h0_im, prm["abar_re"], prm["abar_im"], prm["wb"], prm["wc"], prm["d"],
      prm["w_glu"], prm["b_glu"])


def _s5_params(a_re, a_im, log_dt, b_re, b_im, c_re, c_im, d_skip, w_glu, b_glu):
    dt = jnp.exp(log_dt)[:, None]
    mag = jnp.exp(a_re * dt)
    abar_re, abar_im = mag * jnp.cos(a_im * dt), mag * jnp.sin(a_im * dt)
    nr, ni = abar_re - 1.0, abar_im
    den = a_re * a_re + a_im * a_im
    s_re = (nr * a_re + ni * a_im) / den
    s_im = (ni * a_re - nr * a_im) / den
    bb_re = s_re[..., None] * b_re - s_im[..., None] * b_im
    bb_im = s_re[..., None] * b_im + s_im[..., None] * b_re
    eye = jnp.eye(SSM_GROUPS, dtype=F32)
    wb_re = jnp.einsum('gpk,gh->gkhp', bb_re, eye).reshape(SSM_WIDTH, N_STATE)
    wb_im = jnp.einsum('gpk,gh->gkhp', bb_im, eye).reshape(SSM_WIDTH, N_STATE)
    wc_re = jnp.einsum('gkp,gh->gphk', c_re, eye).reshape(N_STATE, SSM_WIDTH)
    wc_im = jnp.einsum('gkp,gh->gphk', c_im, eye).reshape(N_STATE, SSM_WIDTH)
    hi_, hs = SSM_WIDTH // 2, N_STATE // 2
    wb = jnp.stack([jnp.concatenate([wb_re[h * hi_:(h + 1) * hi_, h * hs:(h + 1) * hs],
                                     wb_im[h * hi_:(h + 1) * hi_, h * hs:(h + 1) * hs]], axis=1)
                    for h in range(2)])
    wc = jnp.stack([jnp.concatenate([wc_re[h * hs:(h + 1) * hs, h * hi_:(h + 1) * hi_],
                                     -wc_im[h * hs:(h + 1) * hs, h * hi_:(h + 1) * hi_]], axis=0)
                    for h in range(2)])
    return {"abar_re": abar_re.reshape(1, N_STATE), "abar_im": abar_im.reshape(1, N_STATE),
            "wb": wb.astype(BF16), "wc": wc.astype(BF16), "d": d_skip.reshape(1, SSM_WIDTH),
            "w_glu": w_glu.astype(BF16), "b_glu": b_glu.reshape(1, SSM_WIDTH)}


def _from_key_on_matrix(n):
    m = lax.broadcasted_iota(jnp.int32, (n, n), 0)
    j = lax.broadcasted_iota(jnp.int32, (n, n), 1)
    return jnp.where(m >= j, 1.0, 0.0).astype(BF16)


def _softplus(z, mask):
    p = jnp.maximum(z, 0.0) + jnp.log(1.0 + jnp.exp(-jnp.abs(z)))
    if mask is not None:
        p = jnp.where(mask, p, 0.0)
    return p.astype(BF16)


def _stick_weights(z, sums, carry, mask):
    w = jnp.exp(z - (sums + carry))
    if mask is not None:
        w = jnp.where(mask, w, 0.0)
    return w.astype(BF16)


def _attn_prompt_kernel(bias_ref, q_ref, k_ref, v_ref, o_ref, kb_ref, vb_ref, from_ref, qm_ref, z_ref, p_ref,
                        w_ref, car_ref, acc_ref, *, seq_len):
    t_pad = kb_ref.shape[0]
    for src, dst in ((k_ref, kb_ref), (v_ref, vb_ref)):
        dst[0:seq_len, :] = src[...].astype(BF16)
        dst[seq_len:t_pad, :] = jnp.zeros((t_pad - seq_len, ATTN_WIDTH), BF16)
    from_ref[...] = _from_key_on_matrix(K_BLOCK)
    lane_head, _ = _iota_div((1, LANES), 1, HEAD_DIM)
    n_chunks = ATTN_WIDTH // LANES
    chunk = lambda c: slice(c * LANES, (c + 1) * LANES)
    bias = [bias_ref[h] for h in range(N_HEADS)]

    def q_block(r0, tq, k_diag, n_full):
        qi = lax.broadcasted_iota(jnp.int32, (tq, K_BLOCK), 0) + (r0 - k_diag)
        kj = lax.broadcasted_iota(jnp.int32, (tq, K_BLOCK), 1)
        diag_mask = kj < qi
        rows = lambda h: slice(h * Q_BLOCK, h * Q_BLOCK + tq)
        for c in range(n_chunks):
            qc = q_ref[pl.ds(r0, tq), chunk(c)]
            for e in range(2):
                qm_ref[2 * c + e, 0:tq, :] = jnp.where(lane_head == e, qc, 0.0).astype(BF16)

        def sweep(k0, mask, first):
            for h in range(N_HEADS):
                z = _dot_nt(qm_ref[h, 0:tq, :], kb_ref[pl.ds(k0, K_BLOCK), chunk(h // 2)]) + bias[h]
                z_ref[h, 0:tq, :] = z
                p_ref[rows(h), :] = _softplus(z, mask)
            if tq == Q_BLOCK:
                sums = _dot(p_ref[...], from_ref[...])
            for h in range(N_HEADS):
                sums_h = sums[rows(h)] if tq == Q_BLOCK else _dot(p_ref[rows(h), :], from_ref[...])
                carry = jnp.zeros((tq, 1), F32) if first else car_ref[h, 0:tq, :]
                w_ref[h, 0:tq, :] = _stick_weights(z_ref[h, 0:tq, :], sums_h, carry, mask)
                car_ref[h, 0:tq, :] = carry + sums_h[:, 0:1]
            for h in range(N_HEADS):
                pv = _dot(w_ref[h, 0:tq, :], vb_ref[pl.ds(k0, K_BLOCK), chunk(h // 2)])
                acc_ref[h, 0:tq, :] = pv if first else acc_ref[h, 0:tq, :] + pv

        sweep(k_diag, diag_mask, True)

        def full(jj, _):
            sweep(pl.multiple_of(k_diag - (jj + 1) * K_BLOCK, K_BLOCK), None, False)
            return 0

        lax.fori_loop(0, n_full, full, 0)
        for c in range(n_chunks):
            o_ref[pl.ds(r0, tq), chunk(c)] = jnp.where(
                lane_head == 0, acc_ref[2 * c, 0:tq, :], acc_ref[2 * c + 1, 0:tq, :])

    ratio = K_BLOCK // Q_BLOCK
    n_q = seq_len // Q_BLOCK

    def whole(i, _):
        jd = lax.shift_right_logical(i, ratio.bit_length() - 1)
        q_block(pl.multiple_of(i * Q_BLOCK, Q_BLOCK), Q_BLOCK, pl.multiple_of(jd * K_BLOCK, K_BLOCK), jd)
        return 0

    lax.fori_loop(0, n_q, whole, 0)
    if seq_len > n_q * Q_BLOCK:
        r0 = n_q * Q_BLOCK
        q_block(r0, seq_len - r0, r0 // K_BLOCK * K_BLOCK, r0 // K_BLOCK)


def _attn_prompt(q, k, v, bias):
    B, T, _ = q.shape
    assert K_BLOCK % Q_BLOCK == 0 and (K_BLOCK // Q_BLOCK) & (K_BLOCK // Q_BLOCK - 1) == 0
    assert (T % Q_BLOCK) % BF16_ROWS == 0
    t_pad = -(-T // K_BLOCK) * K_BLOCK
    blk = pl.BlockSpec((None, T, ATTN_WIDTH), lambda b: (b, 0, 0))
    return pl.pallas_call(
        functools.partial(_attn_prompt_kernel, seq_len=T),
        grid=(B,),
        in_specs=[pl.BlockSpec(memory_space=pltpu.SMEM), blk, blk, blk],
        out_specs=blk,
        out_shape=jax.ShapeDtypeStruct((B, T, ATTN_WIDTH), F32),
        scratch_shapes=[pltpu.VMEM((t_pad, ATTN_WIDTH), BF16), pltpu.VMEM((t_pad, ATTN_WIDTH), BF16),
                        pltpu.VMEM((K_BLOCK, K_BLOCK), BF16),
                        pltpu.VMEM((N_HEADS, Q_BLOCK, LANES), BF16),
                        pltpu.VMEM((N_HEADS, Q_BLOCK, K_BLOCK), F32),
                        pltpu.VMEM((N_HEADS * Q_BLOCK, K_BLOCK), BF16),
                        pltpu.VMEM((N_HEADS, Q_BLOCK, K_BLOCK), BF16),
                        pltpu.VMEM((N_HEADS, Q_BLOCK, 1), F32),
                        pltpu.VMEM((N_HEADS, Q_BLOCK, LANES), F32)],
        compiler_params=_vmem(56 * MIB),
        name="attn_prompt",
    )(bias, q, k, v)


def _attn_sample_kernel(pt_ref, bias_ref, q_ref, kn_ref, vn_ref, *rest, t_new, page, n_pages):
    k_refs, v_refs = rest[:n_pages], rest[n_pages:2 * n_pages]
    o_ref, pad_k, pad_v = rest[2 * n_pages:]
    rows = N_HEADS * t_new
    from_on = _from_key_on_matrix(page)
    bias = bias_ref[...]
    head_lanes = lambda h: slice(h * HEAD_DIM, (h + 1) * HEAD_DIM)
    qh = [q_ref[:, head_lanes(h)].astype(BF16) for h in range(N_HEADS)]

    def head_rows(ref):
        by_head = jnp.swapaxes(ref[...], 0, 1).astype(BF16)
        return lambda h: by_head[h]

    for pad, new in ((pad_k, kn_ref), (pad_v, vn_ref)):
        pad[...] = jnp.zeros_like(pad)
        for h in range(N_HEADS):
            pad[h, 0:t_new, :] = new[:, head_lanes(h)]
    _, tok = _iota_div((rows, page), 0, t_new)
    new_mask = lax.broadcasted_iota(jnp.int32, (rows, page), 1) < tok
    blocks = [(lambda: (lambda h: pad_k[h].astype(BF16)), lambda: (lambda h: pad_v[h].astype(BF16)), new_mask)]
    blocks += [(functools.partial(head_rows, k_refs[a]), functools.partial(head_rows, v_refs[a]), None)
               for a in range(n_pages)]

    scored, values = [], []
    for keys, vals, mask in blocks:
        k_of = keys()
        z = jnp.concatenate([_dot_nt(qh[h], k_of(h)) for h in range(N_HEADS)], axis=0) + bias
        scored.append((z, _dot(_softplus(z, mask), from_on)))
        values.append(vals())
    carry = jnp.zeros((rows, 1), F32)
    acc = [jnp.zeros((t_new, HEAD_DIM), F32)] * N_HEADS
    for (_, _, mask), v_of, (z, sums) in zip(blocks, values, scored):
        w = _stick_weights(z, sums, carry, mask)
        carry = carry + sums[:, 0:1]
        acc = [acc[h] + _dot(w[h * t_new:(h + 1) * t_new], v_of(h)) for h in range(N_HEADS)]
    o_ref[...] = jnp.concatenate(acc, axis=1)


def _attn_sample(q, k_new, v_new, cache_k, cache_v, page_table, bias_rows, layer):
    S, t_new, _ = q.shape
    n_pages = page_table.shape[1]
    page = cache_k.shape[2]
    rows = N_HEADS * t_new
    assert t_new % BF16_ROWS == 0 or t_new == SUBLANES
    tok = pl.BlockSpec((None, t_new, ATTN_WIDTH), lambda s, pt: (s, 0, 0))

    def page_spec(a):
        return pl.BlockSpec((None, None, page, N_HEADS, HEAD_DIM),
                            lambda s, pt: (layer, pt[s, n_pages - 1 - a], 0, 0, 0))

    page_bytes = page * N_HEADS * LANES * 4
    grid_spec = pltpu.PrefetchScalarGridSpec(
        num_scalar_prefetch=1,
        grid=(S,),
        in_specs=[pl.BlockSpec((rows, 1), lambda s, pt: (0, 0)), tok, tok, tok]
        + [page_spec(a) for a in range(n_pages)] * 2,
        out_specs=tok,
        scratch_shapes=[pltpu.VMEM((N_HEADS, page, HEAD_DIM), F32), pltpu.VMEM((N_HEADS, page, HEAD_DIM), F32)],
    )
    return pl.pallas_call(
        functools.partial(_attn_sample_kernel, t_new=t_new, page=page, n_pages=n_pages),
        grid_spec=grid_spec,
        out_shape=jax.ShapeDtypeStruct((S, t_new, ATTN_WIDTH), F32),
        compiler_params=pltpu.CompilerParams(vmem_limit_bytes=4 * n_pages * page_bytes + 16 * MIB,
                                             dimension_semantics=("arbitrary",)),
        name="attn_sample",
    )(page_table, bias_rows, q, k_new, v_new, *([cache_k] * n_pages), *([cache_v] * n_pages))


def _mix_kernel(x_ref, gs_ref, ga_ref, z_ref, o_ref, wbs_ref, wba_ref, wout_ref, g_ref, b_ref, y_ref, *, alpha):
    merged = (gs_ref[...] * _dot(z_ref[...].astype(BF16), wbs_ref[...])
              + ga_ref[...] * _dot(o_ref[...].astype(BF16), wba_ref[...]))
    mix = _dot(merged.astype(BF16), wout_ref[...])
    y_ref[...] = _layer_norm(alpha * x_ref[...] + mix, g_ref[...], b_ref[...])


def _mix(x3, gs, ga, z_tm, o, wbs, wba, wout, ln_g, ln_b, *, tm, alpha):
    G, T, _ = x3.shape
    row = lambda w: pl.BlockSpec((None, tm, w), lambda g, i: (g, i, 0))
    return pl.pallas_call(
        functools.partial(_mix_kernel, alpha=alpha),
        grid=(G, T // tm),
        in_specs=[row(D_MODEL), row(D_MODEL), row(D_MODEL),
                  pl.BlockSpec((tm, SSM_WIDTH), lambda g, i: (i, g)), row(ATTN_WIDTH),
                  _const_spec((SSM_WIDTH, D_MODEL)), _const_spec((ATTN_WIDTH, D_MODEL)),
                  _const_spec((D_MODEL, D_MODEL)), _const_spec((1, D_MODEL)), _const_spec((1, D_MODEL))],
        out_specs=row(D_MODEL),
        out_shape=jax.ShapeDtypeStruct((G, T, D_MODEL), F32),
        compiler_params=_vmem(48 * MIB),
        name="mix",
    )(x3, gs, ga, z_tm, o, wbs, wba, wout, ln_g, ln_b)


def _chunk_major(a):
    *lead, rows, cols = a.shape
    return jnp.swapaxes(a.reshape(*lead, rows, cols // FF_CHUNK, FF_CHUNK), -3, -2)


def _from_chunk_major(a):
    *lead, n, rows, _ = a.shape
    return jnp.swapaxes(a, -3, -2).reshape(*lead, rows, n * FF_CHUNK)


def _ffn_kernel(x_ref, prev_ref, wup_ref, cw_ref, cb_ref, wdown_ref, g_ref, b_ref,
                y_ref, up_ref, acc_ref, xb_ref, ua_ref, ub_ref, *, alpha, seq_len, carry_rows):
    tm = x_ref.shape[0]
    xb_ref[...] = x_ref[...].astype(BF16)
    if seq_len == tm:
        t_loc = lax.broadcasted_iota(jnp.int32, (tm, 1), 0)
    else:
        _, t_loc = _iota_div((tm, 1), 0, seq_len)
    if carry_rows:
        @pl.when(pl.program_id(1) == 0)
        def _():
            up_ref[...] = prev_ref[...]

    def project(c, u_ref):
        u_ref[0] = _dot(xb_ref[...], wup_ref[c])
        u_ref[1] = _dot(xb_ref[...], wup_ref[c + N_FF_CHUNKS])

    def conv(j, up):
        if carry_rows:
            before2 = jnp.broadcast_to(up_ref[j, SUBLANES - 2:SUBLANES - 1, :], (SUBLANES, FF_CHUNK))
            before1 = jnp.broadcast_to(up_ref[j, SUBLANES - 1:SUBLANES, :], (SUBLANES, FF_CHUNK))
            t8 = t_loc[0:SUBLANES]
            up1, up2 = pltpu.roll(up, 1, axis=0), pltpu.roll(up, 2, axis=0)
            head1 = jnp.where(t8 >= 1, up1[0:SUBLANES], before1)
            head2 = jnp.where(t8 >= 2, up2[0:SUBLANES], jnp.where(t8 == 0, before2, before1))
            up1 = jnp.concatenate([head1, up1[SUBLANES:]], axis=0)
            up2 = jnp.concatenate([head2, up2[SUBLANES:]], axis=0)
            up_ref[j] = up[tm - SUBLANES:tm, :]
        else:
            prev = prev_ref[j]
            up1 = jnp.where(t_loc >= 1, pltpu.roll(up, 1, axis=0), pltpu.roll(prev, tm - 1, axis=0))
            up2 = jnp.where(t_loc >= 2, pltpu.roll(up, 2, axis=0), prev)
            up_ref[j] = up
        cw = cw_ref[j]
        return cb_ref[j] + cw[0:1] * up2 + cw[1:2] * up1 + cw[2:3] * up

    def down(c, u_ref):
        h = _gelu(conv(c, u_ref[0])) * conv(c + N_FF_CHUNKS, u_ref[1])
        return _dot(h.astype(BF16), wdown_ref[c])

    assert N_FF_CHUNKS % 2 == 1
    project(0, ua_ref)
    acc_ref[...] = jnp.zeros_like(acc_ref)

    def pair(i, _):
        project(2 * i + 1, ub_ref)
        acc_ref[...] += down(2 * i, ua_ref)
        project(2 * i + 2, ua_ref)
        acc_ref[...] += down(2 * i + 1, ub_ref)
        return 0

    lax.fori_loop(0, N_FF_CHUNKS // 2, pair, 0)
    last = down(N_FF_CHUNKS - 1, ua_ref)
    y_ref[...] = _layer_norm(alpha * x_ref[...] + (acc_ref[...] + last), g_ref[...], b_ref[...])


def _ffn(x3, prev, wup, conv_w, conv_b, wdown, ln_g, ln_b, *, tm, alpha, seq_len, carry_rows):
    G, T, _ = x3.shape
    n2 = 2 * N_FF_CHUNKS
    row = lambda w: pl.BlockSpec((None, tm, w), lambda g, i: (g, i, 0))
    if carry_rows:
        up_spec = pl.BlockSpec((None, n2, SUBLANES, FF_CHUNK), lambda g, i: (g, 0, 0, 0))
        up_shape = jax.ShapeDtypeStruct((G, n2, SUBLANES, FF_CHUNK), F32)
    else:
        up_spec = pl.BlockSpec((None, n2, tm, FF_CHUNK), lambda g, i: (g, 0, i, 0))
        up_shape = jax.ShapeDtypeStruct((G, n2, T, FF_CHUNK), F32)
    return pl.pallas_call(
        functools.partial(_ffn_kernel, alpha=alpha, seq_len=seq_len, carry_rows=carry_rows),
        grid=(G, T // tm),
        in_specs=[row(D_MODEL), up_spec, _const_spec((n2, D_MODEL, FF_CHUNK)),
                  _const_spec((n2, CONV_W, FF_CHUNK)), _const_spec((n2, 1, FF_CHUNK)),
                  _const_spec((N_FF_CHUNKS, FF_CHUNK, D_MODEL)), _const_spec((1, D_MODEL)),
                  _const_spec((1, D_MODEL))],
        out_specs=[row(D_MODEL), up_spec],
        out_shape=[jax.ShapeDtypeStruct((G, T, D_MODEL), F32), up_shape],
        scratch_shapes=[pltpu.VMEM((tm, D_MODEL), F32), pltpu.VMEM((tm, D_MODEL), BF16),
                        pltpu.VMEM((2, tm, FF_CHUNK), F32), pltpu.VMEM((2, tm, FF_CHUNK), F32)],
        compiler_params=pltpu.CompilerParams(vmem_limit_bytes=56 * MIB,
                                             dimension_semantics=("arbitrary", "arbitrary")),
        name="conv_ffn",
    )(x3, prev, wup, conv_w, conv_b, wdown, ln_g, ln_b)


def _largest_divisor(n, cap, multiple):
    best = None
    for d in range(multiple, cap + 1, multiple):
        if n % d == 0:
            best = d
    assert best is not None, (n, cap, multiple)
    return best


def _layer_weights(l, w_in, ssm, w_branch_ssm, w_branch_attn, w_out, ln1_g, ln1_b, w_up, conv_w, conv_b,
                   w_down, ln2_g, ln2_b, sb_bias):
    row = lambda a: a[l].reshape(1, -1)
    return {
        "w_in": w_in[l].astype(BF16), "s5": _s5_params(*[a[l] for a in ssm]),
        "wbs": w_branch_ssm[l].astype(BF16), "wba": w_branch_attn[l].astype(BF16),
        "wout": w_out[l].astype(BF16), "ln1_g": row(ln1_g), "ln1_b": row(ln1_b),
        "wup": _chunk_major(w_up[l].astype(BF16)), "conv_w": _chunk_major(conv_w[l]),
        "conv_b": _chunk_major(row(conv_b)),
        "wdown": w_down[l].astype(BF16).reshape(N_FF_CHUNKS, FF_CHUNK, D_MODEL),
        "ln2_g": row(ln2_g), "ln2_b": row(ln2_b), "sb_bias": sb_bias[l],
    }


def kernel(x_prompt, x_sample, cache_k, cache_v, state_ssm_re, state_ssm_im, state_conv, page_table, meta_tokens, ln_in_g, ln_in_b, w_in, ssm_a_re, ssm_a_im, ssm_log_dt, ssm_b_re, ssm_b_im, ssm_c_re, ssm_c_im, ssm_d, w_glu, b_glu, w_branch_ssm, w_branch_attn, w_out, ln1_g, ln1_b, w_up, conv_w, conv_b, w_down, ln2_g, ln2_b, sb_bias):
    B, seq, _ = x_prompt.shape
    S, t_new, _ = x_sample.shape
    depth = w_in.shape[0]
    alpha = (2.0 * depth) ** 0.25
    T = N_META + seq
    assert B % SUBLANES == 0 and S % SUBLANES == 0 and seq % Q_BLOCK == 0 and t_new >= CONV_W - 1
    assert cache_k.shape[3:] == (N_HEADS, HEAD_DIM)

    ssm = (ssm_a_re, ssm_a_im, ssm_log_dt, ssm_b_re, ssm_b_im, ssm_c_re, ssm_c_im, ssm_d, w_glu, b_glu)
    lng, lnb = ln_in_g.reshape(1, -1), ln_in_b.reshape(1, -1)

    tm_p = _largest_divisor(T, PROMPT_ROWS_CAP, BF16_ROWS)
    tc_p = _largest_divisor(T, SCAN_STEPS_CAP, 2)
    rows_s = S * t_new
    tm_s = _largest_divisor(rows_s, SAMPLE_ROWS_CAP, BF16_ROWS * t_new)

    meta = jnp.broadcast_to(meta_tokens[None], (B, N_META, D_MODEL))
    xp = jnp.concatenate([meta, x_prompt], axis=1)
    xs = x_sample.reshape(1, rows_s, D_MODEL)
    zeros_state = jnp.zeros((B, N_STATE), F32)
    zeros_conv = jnp.zeros((B, 2 * N_FF_CHUNKS, SUBLANES, FF_CHUNK), F32)
    to_tm = lambda a: a.reshape(S, t_new, -1).transpose(1, 0, 2).reshape(rows_s, -1)
    from_tm = lambda a: a.reshape(t_new, S, -1).transpose(1, 0, 2).reshape(rows_s, -1)

    outs_p, outs_s = [], []
    for l in range(depth):
        w = _layer_weights(l, w_in, ssm, w_branch_ssm, w_branch_attn, w_out, ln1_g, ln1_b, w_up, conv_w,
                           conv_b, w_down, ln2_g, ln2_b, sb_bias)
        first = l == 0

        res = _in_proj(xp, lng, lnb, w["w_in"], tm=tm_p, ln_input=first)
        if first:
            xp, *res = res
        u, q, k, v, gs, ga = res
        z, hr, hi = _s5(u.reshape(T * B, SSM_WIDTH), zeros_state, zeros_state, w["s5"], nb=B, tc=tc_p)
        o = _attn_prompt(q, k, v, w["sb_bias"])
        x1 = _mix(xp, gs, ga, z.reshape(T, B * SSM_WIDTH), o, w["wbs"], w["wba"], w["wout"],
                  w["ln1_g"], w["ln1_b"], tm=tm_p, alpha=alpha)
        xp, tail = _ffn(x1, zeros_conv, w["wup"], w["conv_w"], w["conv_b"], w["wdown"], w["ln2_g"],
                        w["ln2_b"], tm=tm_p, alpha=alpha, seq_len=tm_p, carry_rows=True)
        outs_p.append((k.reshape(B, T, N_HEADS, HEAD_DIM), v.reshape(B, T, N_HEADS, HEAD_DIM),
                       hr.reshape(B, SSM_GROUPS, SSM_STATE), hi.reshape(B, SSM_GROUPS, SSM_STATE),
                       _from_chunk_major(tail)[:, SUBLANES - (CONV_W - 1):]))

        res = _in_proj(xs, lng, lnb, w["w_in"], tm=tm_s, ln_input=first)
        if first:
            xs, *res = res
        u, q, k, v, gs, ga = res
        z, hr, hi = _s5(to_tm(u), state_ssm_re[l].reshape(S, N_STATE), state_ssm_im[l].reshape(S, N_STATE),
                        w["s5"], nb=S, tc=t_new)
        k3, v3 = k.reshape(S, t_new, ATTN_WIDTH), v.reshape(S, t_new, ATTN_WIDTH)
        bias_rows = jnp.repeat(w["sb_bias"], t_new).reshape(N_HEADS * t_new, 1)
        o = _attn_sample(q.reshape(S, t_new, ATTN_WIDTH), k3, v3, cache_k, cache_v, page_table, bias_rows, l)
        x1 = _mix(xs, gs, ga, from_tm(z), o.reshape(1, rows_s, ATTN_WIDTH), w["wbs"], w["wba"], w["wout"],
                  w["ln1_g"], w["ln1_b"], tm=tm_s, alpha=alpha)
        prev = jnp.pad(state_conv[l], ((0, 0), (0, t_new - (CONV_W - 1)), (0, 0))).reshape(1, rows_s, 2 * D_FF)
        xs, up = _ffn(x1, _chunk_major(prev), w["wup"], w["conv_w"], w["conv_b"], w["wdown"], w["ln2_g"],
                      w["ln2_b"], tm=tm_s, alpha=alpha, seq_len=t_new, carry_rows=False)
        up = up.reshape(2 * N_FF_CHUNKS, S, t_new, FF_CHUNK)[:, :, t_new - (CONV_W - 1):]
        outs_s.append((k3.reshape(S, t_new, N_HEADS, HEAD_DIM), v3.reshape(S, t_new, N_HEADS, HEAD_DIM),
                       hr.reshape(S, SSM_GROUPS, SSM_STATE), hi.reshape(S, SSM_GROUPS, SSM_STATE),
                       up.transpose(1, 2, 0, 3).reshape(S, CONV_W - 1, 2 * D_FF)))

    stack = lambda outs, j: jnp.stack([o[j] for o in outs])
    return (xp[:, N_META:], xs.reshape(S, t_new, D_MODEL),
            *[stack(outs_p, j) for j in range(5)], *[stack(outs_s, j) for j in range(5)])
```

```python
import functools
import math

import jax
import jax.numpy as jnp
from jax import lax
from jax.experimental import pallas as pl
from jax.experimental.pallas import tpu as pltpu

F32 = jnp.float32
BF16 = jnp.bfloat16

D_MODEL = 1024
N_META = 16
SSM_WIDTH = 512
SSM_GROUP = 16
SSM_GROUPS = 32
SSM_STATE = 64
N_STATE = SSM_GROUPS * SSM_STATE
HEAD_DIM = 64
N_HEADS = 8
ATTN_WIDTH = 512
D_FF = 2816
CONV_W = 3
LN_EPS = 1e-5
O_Q = SSM_WIDTH
O_K = O_Q + ATTN_WIDTH
O_V = O_K + ATTN_WIDTH
O_GS = O_V + ATTN_WIDTH
O_GA = O_GS + D_MODEL
IN_COLS = O_GA + D_MODEL

SUBLANES = 8
LANES = 128
BF16_ROWS = 16
MXU_COLS = 256
Q_BLOCK = 256
K_BLOCK = 256
FF_CHUNK = MXU_COLS
N_FF_CHUNKS = D_FF // FF_CHUNK
MIB = 1024 * 1024
PROMPT_ROWS_CAP = 700
SAMPLE_ROWS_CAP = 256
SCAN_STEPS_CAP = 96


def _vmem(nbytes):
    return pltpu.CompilerParams(vmem_limit_bytes=int(nbytes))


def _const_spec(shape):
    nd = len(shape)
    return pl.BlockSpec(shape, lambda *_: (0,) * nd, pipeline_mode=pl.Buffered(1))


def _layer_norm(x, g, b):
    mu = jnp.mean(x, axis=-1, keepdims=True)
    xc = x - mu
    var = jnp.mean(xc * xc, axis=-1, keepdims=True)
    return xc * lax.rsqrt(var + LN_EPS) * g + b


def _gelu(x):
    return 0.5 * x * (1.0 + lax.erf(x * math.sqrt(0.5)))


def _iota_div(shape, axis, n):
    assert n & (n - 1) == 0, n
    i = lax.broadcasted_iota(jnp.int32, shape, axis)
    return lax.shift_right_logical(i, n.bit_length() - 1), lax.bitwise_and(i, n - 1)


def _dot(a, b):
    return jnp.dot(a, b, preferred_element_type=F32)


def _dot_nt(a, b):
    return lax.dot_general(a, b, (((1,), (1,)), ((), ())), preferred_element_type=F32)


def _in_proj_kernel(x_ref, g_ref, b_ref, w_ref, *out_refs, ln_input):
    if ln_input:
        xn_ref, u_ref, q_ref, k_ref, v_ref, gs_ref, ga_ref = out_refs
    else:
        u_ref, q_ref, k_ref, v_ref, gs_ref, ga_ref = out_refs
    x = x_ref[...]
    if ln_input:
        x = _layer_norm(x, g_ref[...], b_ref[...])
        xn_ref[...] = x
    xb = x.astype(BF16)
    u_ref[...] = _dot(xb, w_ref[:, 0:O_Q])
    q_ref[...] = _dot(xb, w_ref[:, O_Q:O_K]) * (HEAD_DIM ** -0.5)
    k_ref[...] = _dot(xb, w_ref[:, O_K:O_V])
    v_ref[...] = _dot(xb, w_ref[:, O_V:O_GS])
    gs_ref[...] = jax.nn.sigmoid(_dot(xb, w_ref[:, O_GS:O_GA]))
    ga_ref[...] = jax.nn.sigmoid(_dot(xb, w_ref[:, O_GA:IN_COLS]))


def _in_proj(x3, ln_g, ln_b, w_in_bf, *, tm, ln_input):
    G, T, _ = x3.shape
    row = lambda w: pl.BlockSpec((None, tm, w), lambda g, i: (g, i, 0))
    out_shape, out_specs = [], []
    if ln_input:
        out_shape.append(jax.ShapeDtypeStruct((G, T, D_MODEL), F32))
        out_specs.append(row(D_MODEL))
    out_shape.append(jax.ShapeDtypeStruct((T, G * SSM_WIDTH), F32))
    out_specs.append(pl.BlockSpec((tm, SSM_WIDTH), lambda g, i: (i, g)))
    for _ in range(3):
        out_shape.append(jax.ShapeDtypeStruct((G, T, ATTN_WIDTH), F32))
        out_specs.append(row(ATTN_WIDTH))
    for _ in range(2):
        out_shape.append(jax.ShapeDtypeStruct((G, T, D_MODEL), F32))
        out_specs.append(row(D_MODEL))
    return pl.pallas_call(
        functools.partial(_in_proj_kernel, ln_input=ln_input),
        grid=(G, T // tm),
        in_specs=[row(D_MODEL), _const_spec((1, D_MODEL)), _const_spec((1, D_MODEL)),
                  _const_spec((D_MODEL, IN_COLS))],
        out_specs=out_specs,
        out_shape=out_shape,
        compiler_params=_vmem(56 * MIB),
        name="in_proj",
    )(x3, ln_g, ln_b, w_in_bf)


SCAN_COLS = 512


def _s5_kernel(u_ref, h0r_ref, h0i_ref, ar_ref, ai_ref, wb_ref, wc_ref, d_ref, wglu_ref, bglu_ref,
               z_ref, hr_out, hi_out, xr_ref, xi_ref, hr_s, hi_s, *, nb, tc):
    @pl.when(pl.program_id(0) == 0)
    def _():
        hr_s[...] = h0r_ref[...]
        hi_s[...] = h0i_ref[...]

    half_in = SSM_WIDTH // 2
    half_st = N_STATE // 2
    u = u_ref[...]
    ub = u.astype(BF16)
    for h in range(2):
        ubh = ub[:, h * half_in:(h + 1) * half_in]
        xr_ref[:, h * half_st:(h + 1) * half_st] = _dot(ubh, wb_ref[h, :, :half_st])
        xi_ref[:, h * half_st:(h + 1) * half_st] = _dot(ubh, wb_ref[h, :, half_st:])

    for c in range(N_STATE // SCAN_COLS):
        cols = slice(c * SCAN_COLS, (c + 1) * SCAN_COLS)
        ar = jnp.broadcast_to(ar_ref[:, cols], (SUBLANES, SCAN_COLS))
        ai = jnp.broadcast_to(ai_ref[:, cols], (SUBLANES, SCAN_COLS))

        def seq_group(gi, _):
            r0 = pl.multiple_of(gi * SUBLANES, SUBLANES)

            def step(t, carry):
                hr, hi = carry
                r = pl.multiple_of(t * nb + r0, SUBLANES)
                nhr = ar * hr - ai * hi + xr_ref[pl.ds(r, SUBLANES), cols]
                nhi = ar * hi + ai * hr + xi_ref[pl.ds(r, SUBLANES), cols]
                xr_ref[pl.ds(r, SUBLANES), cols] = nhr
                xi_ref[pl.ds(r, SUBLANES), cols] = nhi
                return nhr, nhi

            hr, hi = lax.fori_loop(0, tc, step,
                                   (hr_s[pl.ds(r0, SUBLANES), cols], hi_s[pl.ds(r0, SUBLANES), cols]),
                                   unroll=2)
            hr_s[pl.ds(r0, SUBLANES), cols] = hr
            hi_s[pl.ds(r0, SUBLANES), cols] = hi
            return 0

        lax.fori_loop(0, nb // SUBLANES, seq_group, 0)

    ys = []
    for h in range(2):
        st = slice(h * half_st, (h + 1) * half_st)
        y = (_dot(xr_ref[:, st].astype(BF16), wc_ref[h, :half_st, :])
             + _dot(xi_ref[:, st].astype(BF16), wc_ref[h, half_st:, :]))
        ch = slice(h * half_in, (h + 1) * half_in)
        ys.append(y + u[:, ch] * d_ref[:, ch])
    y = jnp.concatenate(ys, axis=1)
    g = _gelu(y)
    z_ref[...] = g * jax.nn.sigmoid(_dot(g.astype(BF16), wglu_ref[...]) + bglu_ref[...])
    hr_out[...] = hr_s[...]
    hi_out[...] = hi_s[...]


def _s5(u_tm, h0_re, h0_im, prm, *, nb, tc):
    rows_total = u_tm.shape[0]
    rows = tc * nb
    steps = rows_total // rows
    blk = pl.BlockSpec((rows, SSM_WIDTH), lambda i: (i, 0))
    st = _const_spec((nb, N_STATE))
    return pl.pallas_call(
        functools.partial(_s5_kernel, nb=nb, tc=tc),
        grid=(steps,),
        in_specs=[blk, st, st, _const_spec((1, N_STATE)), _const_spec((1, N_STATE)),
                  _const_spec((2, SSM_WIDTH // 2, N_STATE)), _const_spec((2, N_STATE, SSM_WIDTH // 2)),
                  _const_spec((1, SSM_WIDTH)), _const_spec((SSM_WIDTH, SSM_WIDTH)),
                  _const_spec((1, SSM_WIDTH))],
        out_specs=[blk, pl.BlockSpec((nb, N_STATE), lambda i: (0, 0)),
                   pl.BlockSpec((nb, N_STATE), lambda i: (0, 0))],
        out_shape=[jax.ShapeDtypeStruct((rows_total, SSM_WIDTH), F32),
                   jax.ShapeDtypeStruct((nb, N_STATE), F32),
                   jax.ShapeDtypeStruct((nb, N_STATE), F32)],
        scratch_shapes=[pltpu.VMEM((rows, N_STATE), F32), pltpu.VMEM((rows, N_STATE), F32),
                        pltpu.VMEM((nb, N_STATE), F32), pltpu.VMEM((nb, N_STATE), F32)],
        compiler_params=pltpu.CompilerParams(vmem_limit_bytes=56 * MIB,
                                             dimension_semantics=("arbitrary",)),
        name="s5_scan",
    )(u_tm, h0_re, h0_im, prm["abar_re"], prm["abar_im"], prm["wb"], prm["wc"], prm["d"],
      prm["w_glu"], prm["b_glu"])


def _s5_params(a_re, a_im, log_dt, b_re, b_im, c_re, c_im, d_skip, w_glu, b_glu):
    dt = jnp.exp(log_dt)[:, None]
    mag = jnp.exp(a_re * dt)
    abar_re, abar_im = mag * jnp.cos(a_im * dt), mag * jnp.sin(a_im * dt)
    nr, ni = abar_re - 1.0, abar_im
    den = a_re * a_re + a_im * a_im
    s_re = (nr * a_re + ni * a_im) / den
    s_im = (ni * a_re - nr * a_im) / den
    bb_re = s_re[..., None] * b_re - s_im[..., None] * b_im
    bb_im = s_re[..., None] * b_im + s_im[..., None] * b_re
    eye = jnp.eye(SSM_GROUPS, dtype=F32)
    wb_re = jnp.einsum('gpk,gh->gkhp', bb_re, eye).reshape(SSM_WIDTH, N_STATE)
    wb_im = jnp.einsum('gpk,gh->gkhp', bb_im, eye).reshape(SSM_WIDTH, N_STATE)
    wc_re = jnp.einsum('gkp,gh->gphk', c_re, eye).reshape(N_STATE, SSM_WIDTH)
    wc_im = jnp.einsum('gkp,gh->gphk', c_im, eye).reshape(N_STATE, SSM_WIDTH)
    hi_, hs = SSM_WIDTH // 2, N_STATE // 2
    wb = jnp.stack([jnp.concatenate([wb_re[h * hi_:(h + 1) * hi_, h * hs:(h + 1) * hs],
                                     wb_im[h * hi_:(h + 1) * hi_, h * hs:(h + 1) * hs]], axis=1)
                    for h in range(2)])
    wc = jnp.stack([jnp.concatenate([wc_re[h * hs:(h + 1) * hs, h * hi_:(h + 1) * hi_],
                                     -wc_im[h * hs:(h + 1) * hs, h * hi_:(h + 1) * hi_]], axis=0)
                    for h in range(2)])
    return {"abar_re": abar_re.reshape(1, N_STATE), "abar_im": abar_im.reshape(1, N_STATE),
            "wb": wb.astype(BF16), "wc": wc.astype(BF16), "d": d_skip.reshape(1, SSM_WIDTH),
            "w_glu": w_glu.astype(BF16), "b_glu": b_glu.reshape(1, SSM_WIDTH)}


def _from_key_on_matrix(n):
    m = lax.broadcasted_iota(jnp.int32, (n, n), 0)
    j = lax.broadcasted_iota(jnp.int32, (n, n), 1)
    return jnp.where(m >= j, 1.0, 0.0).astype(BF16)


def _softplus(z, mask):
    p = jnp.maximum(z, 0.0) + jnp.log(1.0 + jnp.exp(-jnp.abs(z)))
    if mask is not None:
        p = jnp.where(mask, p, 0.0)
    return p.astype(BF16)


def _stick_weights(z, sums, carry, mask):
    w = jnp.exp(z - (sums + carry))
    if mask is not None:
        w = jnp.where(mask, w, 0.0)
    return w.astype(BF16)


def _attn_prompt_kernel(bias_ref, q_ref, k_ref, v_ref, o_ref, kb_ref, vb_ref, from_ref, qm_ref, z_ref, p_ref,
                        w_ref, car_ref, acc_ref, *, seq_len):
    t_pad = kb_ref.shape[0]
    for src, dst in ((k_ref, kb_ref), (v_ref, vb_ref)):
        dst[0:seq_len, :] = src[...].astype(BF16)
        dst[seq_len:t_pad, :] = jnp.zeros((t_pad - seq_len, ATTN_WIDTH), BF16)
    from_ref[...] = _from_key_on_matrix(K_BLOCK)
    lane_head, _ = _iota_div((1, LANES), 1, HEAD_DIM)
    n_chunks = ATTN_WIDTH // LANES
    chunk = lambda c: slice(c * LANES, (c + 1) * LANES)
    bias = [bias_ref[h] for h in range(N_HEADS)]

    def q_block(r0, tq, k_diag, n_full):
        qi = lax.broadcasted_iota(jnp.int32, (tq, K_BLOCK), 0) + (r0 - k_diag)
        kj = lax.broadcasted_iota(jnp.int32, (tq, K_BLOCK), 1)
        diag_mask = kj < qi
        rows = lambda h: slice(h * Q_BLOCK, h * Q_BLOCK + tq)
        for c in range(n_chunks):
            qc = q_ref[pl.ds(r0, tq), chunk(c)]
            for e in range(2):
                qm_ref[2 * c + e, 0:tq, :] = jnp.where(lane_head == e, qc, 0.0).astype(BF16)

        def sweep(k0, mask, first):
            for h in range(N_HEADS):
                z = _dot_nt(qm_ref[h, 0:tq, :], kb_ref[pl.ds(k0, K_BLOCK), chunk(h // 2)]) + bias[h]
                z_ref[h, 0:tq, :] = z
                p_ref[rows(h), :] = _softplus(z, mask)
            if tq == Q_BLOCK:
                sums = _dot(p_ref[...], from_ref[...])
            for h in range(N_HEADS):
                sums_h = sums[rows(h)] if tq == Q_BLOCK else _dot(p_ref[rows(h), :], from_ref[...])
                carry = jnp.zeros((tq, 1), F32) if first else car_ref[h, 0:tq, :]
                w_ref[h, 0:tq, :] = _stick_weights(z_ref[h, 0:tq, :], sums_h, carry, mask)
                car_ref[h, 0:tq, :] = carry + sums_h[:, 0:1]
            for h in range(N_HEADS):
                pv = _dot(w_ref[h, 0:tq, :], vb_ref[pl.ds(k0, K_BLOCK), chunk(h // 2)])
                acc_ref[h, 0:tq, :] = pv if first else acc_ref[h, 0:tq, :] + pv

        sweep(k_diag, diag_mask, True)

        def full(jj, _):
            sweep(pl.multiple_of(k_diag - (jj + 1) * K_BLOCK, K_BLOCK), None, False)
            return 0

        lax.fori_loop(0, n_full, full, 0)
        for c in range(n_chunks):
            o_ref[pl.ds(r0, tq), chunk(c)] = jnp.where(
                lane_head == 0, acc_ref[2 * c, 0:tq, :], acc_ref[2 * c + 1, 0:tq, :])

    ratio = K_BLOCK // Q_BLOCK
    n_q = seq_len // Q_BLOCK

    def whole(i, _):
        jd = lax.shift_right_logical(i, ratio.bit_length() - 1)
        q_block(pl.multiple_of(i * Q_BLOCK, Q_BLOCK), Q_BLOCK, pl.multiple_of(jd * K_BLOCK, K_BLOCK), jd)
        return 0

    lax.fori_loop(0, n_q, whole, 0)
    if seq_len > n_q * Q_BLOCK:
        r0 = n_q * Q_BLOCK
        q_block(r0, seq_len - r0, r0 // K_BLOCK * K_BLOCK, r0 // K_BLOCK)


def _attn_prompt(q, k, v, bias):
    B, T, _ = q.shape
    assert K_BLOCK % Q_BLOCK == 0 and (K_BLOCK // Q_BLOCK) & (K_BLOCK // Q_BLOCK - 1) == 0
    assert (T % Q_BLOCK) % BF16_ROWS == 0
    t_pad = -(-T // K_BLOCK) * K_BLOCK
    blk = pl.BlockSpec((None, T, ATTN_WIDTH), lambda b: (b, 0, 0))
    return pl.pallas_call(
        functools.partial(_attn_prompt_kernel, seq_len=T),
        grid=(B,),
        in_specs=[pl.BlockSpec(memory_space=pltpu.SMEM), blk, blk, blk],
        out_specs=blk,
        out_shape=jax.ShapeDtypeStruct((B, T, ATTN_WIDTH), F32),
        scratch_shapes=[pltpu.VMEM((t_pad, ATTN_WIDTH), BF16), pltpu.VMEM((t_pad, ATTN_WIDTH), BF16),
                        pltpu.VMEM((K_BLOCK, K_BLOCK), BF16),
                        pltpu.VMEM((N_HEADS, Q_BLOCK, LANES), BF16),
                        pltpu.VMEM((N_HEADS, Q_BLOCK, K_BLOCK), F32),
                        pltpu.VMEM((N_HEADS * Q_BLOCK, K_BLOCK), BF16),
                        pltpu.VMEM((N_HEADS, Q_BLOCK, K_BLOCK), BF16),
                        pltpu.VMEM((N_HEADS, Q_BLOCK, 1), F32),
                        pltpu.VMEM((N_HEADS, Q_BLOCK, LANES), F32)],
        compiler_params=_vmem(56 * MIB),
        name="attn_prompt",
    )(bias, q, k, v)


def _attn_sample_kernel(pt_ref, bias_ref, q_ref, kn_ref, vn_ref, *rest, t_new, page, n_pages):
    k_refs, v_refs = rest[:n_pages], rest[n_pages:2 * n_pages]
    o_ref, pad_k, pad_v = rest[2 * n_pages:]
    rows = N_HEADS * t_new
    from_on = _from_key_on_matrix(page)
    bias = bias_ref[...]
    row_head, _ = _iota_div((rows, ATTN_WIDTH), 0, t_new)
    lane_head, _ = _iota_div((rows, ATTN_WIDTH), 1, HEAD_DIM)
    own = row_head == lane_head
    qx = jnp.where(own, jnp.concatenate([q_ref[...]] * N_HEADS, axis=0), 0.0).astype(BF16)

    for pad, new in ((pad_k, kn_ref), (pad_v, vn_ref)):
        pad[...] = jnp.zeros_like(pad)
        pad[0:t_new, :] = new[...]
    _, tok = _iota_div((rows, page), 0, t_new)
    new_mask = lax.broadcasted_iota(jnp.int32, (rows, page), 1) < tok

    def scores(s, mask):
        z = s + bias
        return z, _dot(_softplus(z, mask), from_on), mask

    flat = lambda ref: ref[...].reshape(ATTN_WIDTH, page).astype(BF16)
    scored = [scores(_dot_nt(qx, pad_k[...].astype(BF16)), new_mask)]
    scored += [scores(_dot(qx, flat(k_refs[a])), None) for a in range(n_pages)]
    carry = jnp.zeros((rows, 1), F32)
    acc = jnp.zeros((rows, ATTN_WIDTH), F32)
    for i, (z, sums, mask) in enumerate(scored):
        w = _stick_weights(z, sums, carry, mask)
        carry = carry + sums[:, 0:1]
        acc = acc + (_dot(w, pad_v[...].astype(BF16)) if i == 0 else _dot_nt(w, flat(v_refs[i - 1])))
    acc = jnp.where(own, acc, 0.0)
    out = acc[0:t_new]
    for h in range(1, N_HEADS):
        out = out + acc[h * t_new:(h + 1) * t_new]
    o_ref[...] = out


def _attn_sample(q, k_new, v_new, cache_k, cache_v, page_table, bias_rows, layer):
    S, t_new, _ = q.shape
    n_pages = page_table.shape[1]
    page = cache_k.shape[4]
    rows = N_HEADS * t_new
    assert t_new % BF16_ROWS == 0 or t_new == SUBLANES
    tok = pl.BlockSpec((None, t_new, ATTN_WIDTH), lambda s, pt: (s, 0, 0))

    def page_spec(a):
        return pl.BlockSpec((None, None, N_HEADS, HEAD_DIM, page),
                            lambda s, pt: (layer, pt[s, n_pages - 1 - a], 0, 0, 0))

    page_bytes = N_HEADS * HEAD_DIM * page * 4
    grid_spec = pltpu.PrefetchScalarGridSpec(
        num_scalar_prefetch=1,
        grid=(S,),
        in_specs=[pl.BlockSpec((rows, 1), lambda s, pt: (0, 0)), tok, tok, tok]
        + [page_spec(a) for a in range(n_pages)] * 2,
        out_specs=tok,
        scratch_shapes=[pltpu.VMEM((page, ATTN_WIDTH), F32), pltpu.VMEM((page, ATTN_WIDTH), F32)],
    )
    return pl.pallas_call(
        functools.partial(_attn_sample_kernel, t_new=t_new, page=page, n_pages=n_pages),
        grid_spec=grid_spec,
        out_shape=jax.ShapeDtypeStruct((S, t_new, ATTN_WIDTH), F32),
        compiler_params=pltpu.CompilerParams(vmem_limit_bytes=4 * n_pages * page_bytes + 16 * MIB,
                                             dimension_semantics=("arbitrary",)),
        name="attn_sample",
    )(page_table, bias_rows, q, k_new, v_new, *([cache_k] * n_pages), *([cache_v] * n_pages))


def _mix_kernel(x_ref, gs_ref, ga_ref, z_ref, o_ref, wbs_ref, wba_ref, wout_ref, g_ref, b_ref, y_ref, *, alpha):
    merged = (gs_ref[...] * _dot(z_ref[...].astype(BF16), wbs_ref[...])
              + ga_ref[...] * _dot(o_ref[...].astype(BF16), wba_ref[...]))
    mix = _dot(merged.astype(BF16), wout_ref[...])
    y_ref[...] = _layer_norm(alpha * x_ref[...] + mix, g_ref[...], b_ref[...])


def _mix(x3, gs, ga, z_tm, o, wbs, wba, wout, ln_g, ln_b, *, tm, alpha):
    G, T, _ = x3.shape
    row = lambda w: pl.BlockSpec((None, tm, w), lambda g, i: (g, i, 0))
    return pl.pallas_call(
        functools.partial(_mix_kernel, alpha=alpha),
        grid=(G, T // tm),
        in_specs=[row(D_MODEL), row(D_MODEL), row(D_MODEL),
                  pl.BlockSpec((tm, SSM_WIDTH), lambda g, i: (i, g)), row(ATTN_WIDTH),
                  _const_spec((SSM_WIDTH, D_MODEL)), _const_spec((ATTN_WIDTH, D_MODEL)),
                  _const_spec((D_MODEL, D_MODEL)), _const_spec((1, D_MODEL)), _const_spec((1, D_MODEL))],
        out_specs=row(D_MODEL),
        out_shape=jax.ShapeDtypeStruct((G, T, D_MODEL), F32),
        compiler_params=_vmem(48 * MIB),
        name="mix",
    )(x3, gs, ga, z_tm, o, wbs, wba, wout, ln_g, ln_b)


def _chunk_major(a):
    *lead, rows, cols = a.shape
    return jnp.swapaxes(a.reshape(*lead, rows, cols // FF_CHUNK, FF_CHUNK), -3, -2)


def _from_chunk_major(a):
    *lead, n, rows, _ = a.shape
    return jnp.swapaxes(a, -3, -2).reshape(*lead, rows, n * FF_CHUNK)


def _ffn_kernel(x_ref, prev_ref, wup_ref, cw_ref, cb_ref, wdown_ref, g_ref, b_ref,
                y_ref, up_ref, acc_ref, xb_ref, ua_ref, ub_ref, *, alpha, seq_len, carry_rows):
    tm = x_ref.shape[0]
    xb_ref[...] = x_ref[...].astype(BF16)
    if seq_len == tm:
        t_loc = lax.broadcasted_iota(jnp.int32, (tm, 1), 0)
    else:
        _, t_loc = _iota_div((tm, 1), 0, seq_len)
    if carry_rows:
        @pl.when(pl.program_id(1) == 0)
        def _():
            up_ref[...] = prev_ref[...]

    def project(c, u_ref):
        u_ref[0] = _dot(xb_ref[...], wup_ref[c])
        u_ref[1] = _dot(xb_ref[...], wup_ref[c + N_FF_CHUNKS])

    def conv(j, up):
        if carry_rows:
            before2 = jnp.broadcast_to(up_ref[j, SUBLANES - 2:SUBLANES - 1, :], (SUBLANES, FF_CHUNK))
            before1 = jnp.broadcast_to(up_ref[j, SUBLANES - 1:SUBLANES, :], (SUBLANES, FF_CHUNK))
            t8 = t_loc[0:SUBLANES]
            up1, up2 = pltpu.roll(up, 1, axis=0), pltpu.roll(up, 2, axis=0)
            head1 = jnp.where(t8 >= 1, up1[0:SUBLANES], before1)
            head2 = jnp.where(t8 >= 2, up2[0:SUBLANES], jnp.where(t8 == 0, before2, before1))
            up1 = jnp.concatenate([head1, up1[SUBLANES:]], axis=0)
            up2 = jnp.concatenate([head2, up2[SUBLANES:]], axis=0)
            up_ref[j] = up[tm - SUBLANES:tm, :]
        else:
            prev = prev_ref[j]
            up1 = jnp.where(t_loc >= 1, pltpu.roll(up, 1, axis=0), pltpu.roll(prev, tm - 1, axis=0))
            up2 = jnp.where(t_loc >= 2, pltpu.roll(up, 2, axis=0), prev)
            up_ref[j] = up
        cw = cw_ref[j]
        return cb_ref[j] + cw[0:1] * up2 + cw[1:2] * up1 + cw[2:3] * up

    def down(c, u_ref):
        h = _gelu(conv(c, u_ref[0])) * conv(c + N_FF_CHUNKS, u_ref[1])
        return _dot(h.astype(BF16), wdown_ref[c])

    assert N_FF_CHUNKS % 2 == 1
    project(0, ua_ref)
    acc_ref[...] = jnp.zeros_like(acc_ref)

    def pair(i, _):
        project(2 * i + 1, ub_ref)
        acc_ref[...] += down(2 * i, ua_ref)
        project(2 * i + 2, ua_ref)
        acc_ref[...] += down(2 * i + 1, ub_ref)
        return 0

    lax.fori_loop(0, N_FF_CHUNKS // 2, pair, 0)
    last = down(N_FF_CHUNKS - 1, ua_ref)
    y_ref[...] = _layer_norm(alpha * x_ref[...] + (acc_ref[...] + last), g_ref[...], b_ref[...])


def _ffn(x3, prev, wup, conv_w, conv_b, wdown, ln_g, ln_b, *, tm, alpha, seq_len, carry_rows):
    G, T, _ = x3.shape
    n2 = 2 * N_FF_CHUNKS
    row = lambda w: pl.BlockSpec((None, tm, w), lambda g, i: (g, i, 0))
    if carry_rows:
        up_spec = pl.BlockSpec((None, n2, SUBLANES, FF_CHUNK), lambda g, i: (g, 0, 0, 0))
        up_shape = jax.ShapeDtypeStruct((G, n2, SUBLANES, FF_CHUNK), F32)
    else:
        up_spec = pl.BlockSpec((None, n2, tm, FF_CHUNK), lambda g, i: (g, 0, i, 0))
        up_shape = jax.ShapeDtypeStruct((G, n2, T, FF_CHUNK), F32)
    return pl.pallas_call(
        functools.partial(_ffn_kernel, alpha=alpha, seq_len=seq_len, carry_rows=carry_rows),
        grid=(G, T // tm),
        in_specs=[row(D_MODEL), up_spec, _const_spec((n2, D_MODEL, FF_CHUNK)),
                  _const_spec((n2, CONV_W, FF_CHUNK)), _const_spec((n2, 1, FF_CHUNK)),
                  _const_spec((N_FF_CHUNKS, FF_CHUNK, D_MODEL)), _const_spec((1, D_MODEL)),
                  _const_spec((1, D_MODEL))],
        out_specs=[row(D_MODEL), up_spec],
        out_shape=[jax.ShapeDtypeStruct((G, T, D_MODEL), F32), up_shape],
        scratch_shapes=[pltpu.VMEM((tm, D_MODEL), F32), pltpu.VMEM((tm, D_MODEL), BF16),
                        pltpu.VMEM((2, tm, FF_CHUNK), F32), pltpu.VMEM((2, tm, FF_CHUNK), F32)],
        compiler_params=pltpu.CompilerParams(vmem_limit_bytes=56 * MIB,
                                             dimension_semantics=("arbitrary", "arbitrary")),
        name="conv_ffn",
    )(x3, prev, wup, conv_w, conv_b, wdown, ln_g, ln_b)


def _largest_divisor(n, cap, multiple):
    best = None
    for d in range(multiple, cap + 1, multiple):
        if n % d == 0:
            best = d
    assert best is not None, (n, cap, multiple)
    return best


def _layer_weights(l, w_in, ssm, w_branch_ssm, w_branch_attn, w_out, ln1_g, ln1_b, w_up, conv_w, conv_b,
                   w_down, ln2_g, ln2_b, sb_bias):
    row = lambda a: a[l].reshape(1, -1)
    return {
        "w_in": w_in[l].astype(BF16), "s5": _s5_params(*[a[l] for a in ssm]),
        "wbs": w_branch_ssm[l].astype(BF16), "wba": w_branch_attn[l].astype(BF16),
        "wout": w_out[l].astype(BF16), "ln1_g": row(ln1_g), "ln1_b": row(ln1_b),
        "wup": _chunk_major(w_up[l].astype(BF16)), "conv_w": _chunk_major(conv_w[l]),
        "conv_b": _chunk_major(row(conv_b)),
        "wdown": w_down[l].astype(BF16).reshape(N_FF_CHUNKS, FF_CHUNK, D_MODEL),
        "ln2_g": row(ln2_g), "ln2_b": row(ln2_b), "sb_bias": sb_bias[l],
    }


def kernel(x_prompt, x_sample, cache_k, cache_v, state_ssm_re, state_ssm_im, state_conv, page_table, meta_tokens, ln_in_g, ln_in_b, w_in, ssm_a_re, ssm_a_im, ssm_log_dt, ssm_b_re, ssm_b_im, ssm_c_re, ssm_c_im, ssm_d, w_glu, b_glu, w_branch_ssm, w_branch_attn, w_out, ln1_g, ln1_b, w_up, conv_w, conv_b, w_down, ln2_g, ln2_b, sb_bias):
    B, seq, _ = x_prompt.shape
    S, t_new, _ = x_sample.shape
    depth = w_in.shape[0]
    alpha = (2.0 * depth) ** 0.25
    T = N_META + seq
    assert B % SUBLANES == 0 and S % SUBLANES == 0 and seq % Q_BLOCK == 0 and t_new >= CONV_W - 1
    assert cache_k.shape[3:] == (N_HEADS, HEAD_DIM)
    ck = jnp.transpose(cache_k, (0, 1, 3, 4, 2))
    cv = jnp.transpose(cache_v, (0, 1, 3, 4, 2))

    ssm = (ssm_a_re, ssm_a_im, ssm_log_dt, ssm_b_re, ssm_b_im, ssm_c_re, ssm_c_im, ssm_d, w_glu, b_glu)
    lng, lnb = ln_in_g.reshape(1, -1), ln_in_b.reshape(1, -1)

    tm_p = _largest_divisor(T, PROMPT_ROWS_CAP, BF16_ROWS)
    tc_p = _largest_divisor(T, SCAN_STEPS_CAP, 2)
    rows_s = S * t_new
    tm_s = _largest_divisor(rows_s, SAMPLE_ROWS_CAP, BF16_ROWS * t_new)

    meta = jnp.broadcast_to(meta_tokens[None], (B, N_META, D_MODEL))
    xp = jnp.concatenate([meta, x_prompt], axis=1)
    xs = x_sample.reshape(1, rows_s, D_MODEL)
    zeros_state = jnp.zeros((B, N_STATE), F32)
    zeros_conv = jnp.zeros((B, 2 * N_FF_CHUNKS, SUBLANES, FF_CHUNK), F32)
    to_tm = lambda a: a.reshape(S, t_new, -1).transpose(1, 0, 2).reshape(rows_s, -1)
    from_tm = lambda a: a.reshape(t_new, S, -1).transpose(1, 0, 2).reshape(rows_s, -1)

    outs_p, outs_s = [], []
    for l in range(depth):
        w = _layer_weights(l, w_in, ssm, w_branch_ssm, w_branch_attn, w_out, ln1_g, ln1_b, w_up, conv_w,
                           conv_b, w_down, ln2_g, ln2_b, sb_bias)
        first = l == 0

        res = _in_proj(xp, lng, lnb, w["w_in"], tm=tm_p, ln_input=first)
        if first:
            xp, *res = res
        u, q, k, v, gs, ga = res
        z, hr, hi = _s5(u.reshape(T * B, SSM_WIDTH), zeros_state, zeros_state, w["s5"], nb=B, tc=tc_p)
        o = _attn_prompt(q, k, v, w["sb_bias"])
        x1 = _mix(xp, gs, ga, z.reshape(T, B * SSM_WIDTH), o, w["wbs"], w["wba"], w["wout"],
                  w["ln1_g"], w["ln1_b"], tm=tm_p, alpha=alpha)
        xp, tail = _ffn(x1, zeros_conv, w["wup"], w["conv_w"], w["conv_b"], w["wdown"], w["ln2_g"],
                        w["ln2_b"], tm=tm_p, alpha=alpha, seq_len=tm_p, carry_rows=True)
        outs_p.append((k.reshape(B, T, N_HEADS, HEAD_DIM), v.reshape(B, T, N_HEADS, HEAD_DIM),
                       hr.reshape(B, SSM_GROUPS, SSM_STATE), hi.reshape(B, SSM_GROUPS, SSM_STATE),
                       _from_chunk_major(tail)[:, SUBLANES - (CONV_W - 1):]))

        res = _in_proj(xs, lng, lnb, w["w_in"], tm=tm_s, ln_input=first)
        if first:
            xs, *res = res
        u, q, k, v, gs, ga = res
        z, hr, hi = _s5(to_tm(u), state_ssm_re[l].reshape(S, N_STATE), state_ssm_im[l].reshape(S, N_STATE),
                        w["s5"], nb=S, tc=t_new)
        k3, v3 = k.reshape(S, t_new, ATTN_WIDTH), v.reshape(S, t_new, ATTN_WIDTH)
        bias_rows = jnp.repeat(w["sb_bias"], t_new).reshape(N_HEADS * t_new, 1)
        o = _attn_sample(q.reshape(S, t_new, ATTN_WIDTH), k3, v3, ck, cv, page_table, bias_rows, l)
        x1 = _mix(xs, gs, ga, from_tm(z), o.reshape(1, rows_s, ATTN_WIDTH), w["wbs"], w["wba"], w["wout"],
                  w["ln1_g"], w["ln1_b"], tm=tm_s, alpha=alpha)
        prev = jnp.pad(state_conv[l], ((0, 0), (0, t_new - (CONV_W - 1)), (0, 0))).reshape(1, rows_s, 2 * D_FF)
        xs, up = _ffn(x1, _chunk_major(prev), w["wup"], w["conv_w"], w["conv_b"], w["wdown"], w["ln2_g"],
                      w["ln2_b"], tm=tm_s, alpha=alpha, seq_len=t_new, carry_rows=False)
        up = up.reshape(2 * N_FF_CHUNKS, S, t_new, FF_CHUNK)[:, :, t_new - (CONV_W - 1):]
        outs_s.append((k3.reshape(S, t_new, N_HEADS, HEAD_DIM), v3.reshape(S, t_new, N_HEADS, HEAD_DIM),
                       hr.reshape(S, SSM_GROUPS, SSM_STATE), hi.reshape(S, SSM_GROUPS, SSM_STATE),
                       up.transpose(1, 2, 0, 3).reshape(S, CONV_W - 1, 2 * D_FF)))

    stack = lambda outs, j: jnp.stack([o[j] for o in outs])
    return (xp[:, N_META:], xs.reshape(S, t_new, D_MODEL),
            *[stack(outs_p, j) for j in range(5)], *[stack(outs_s, j) for j in range(5)])
```

```python
import functools
import math

import jax
import jax.numpy as jnp
from jax import lax
from jax.experimental import pallas as pl
from jax.experimental.pallas import tpu as pltpu

F32 = jnp.float32
BF16 = jnp.bfloat16

D_MODEL = 1024
N_META = 16
SSM_WIDTH = 512
SSM_GROUP = 16
SSM_GROUPS = 32
SSM_STATE = 64
N_STATE = SSM_GROUPS * SSM_STATE
HEAD_DIM = 64
N_HEADS = 8
ATTN_WIDTH = 512
D_FF = 2816
CONV_W = 3
LN_EPS = 1e-5
O_Q = SSM_WIDTH
O_K = O_Q + ATTN_WIDTH
O_V = O_K + ATTN_WIDTH
O_GS = O_V + ATTN_WIDTH
O_GA = O_GS + D_MODEL
IN_COLS = O_GA + D_MODEL

SUBLANES = 8
LANES = 128
BF16_ROWS = 16
MXU_COLS = 256
Q_BLOCK = 256
K_BLOCK = 256
FF_CHUNK = MXU_COLS
N_FF_CHUNKS = D_FF // FF_CHUNK
MIB = 1024 * 1024
PROMPT_ROWS_CAP = 700
SAMPLE_ROWS_CAP = 256
SCAN_STEPS_CAP = 96
SAMPLE_SEQS_PER_STEP = 1


def _vmem(nbytes):
    return pltpu.CompilerParams(vmem_limit_bytes=int(nbytes))


def _const_spec(shape):
    nd = len(shape)
    return pl.BlockSpec(shape, lambda *_: (0,) * nd, pipeline_mode=pl.Buffered(1))


def _layer_norm(x, g, b):
    mu = jnp.mean(x, axis=-1, keepdims=True)
    xc = x - mu
    var = jnp.mean(xc * xc, axis=-1, keepdims=True)
    return xc * lax.rsqrt(var + LN_EPS) * g + b


def _gelu(x):
    return 0.5 * x * (1.0 + lax.erf(x * math.sqrt(0.5)))


def _iota_div(shape, axis, n):
    assert n & (n - 1) == 0, n
    i = lax.broadcasted_iota(jnp.int32, shape, axis)
    return lax.shift_right_logical(i, n.bit_length() - 1), lax.bitwise_and(i, n - 1)


def _dot(a, b):
    return jnp.dot(a, b, preferred_element_type=F32)


def _dot_nt(a, b):
    return lax.dot_general(a, b, (((1,), (1,)), ((), ())), preferred_element_type=F32)


def _in_proj_kernel(x_ref, g_ref, b_ref, w_ref, *out_refs, ln_input):
    if ln_input:
        xn_ref, u_ref, q_ref, k_ref, v_ref, gs_ref, ga_ref = out_refs
    else:
        u_ref, q_ref, k_ref, v_ref, gs_ref, ga_ref = out_refs
    x = x_ref[...]
    if ln_input:
        x = _layer_norm(x, g_ref[...], b_ref[...])
        xn_ref[...] = x
    xb = x.astype(BF16)
    u_ref[...] = _dot(xb, w_ref[:, 0:O_Q])
    q_ref[...] = _dot(xb, w_ref[:, O_Q:O_K]) * (HEAD_DIM ** -0.5)
    k_ref[...] = _dot(xb, w_ref[:, O_K:O_V])
    v_ref[...] = _dot(xb, w_ref[:, O_V:O_GS])
    gs_ref[...] = jax.nn.sigmoid(_dot(xb, w_ref[:, O_GS:O_GA]))
    ga_ref[...] = jax.nn.sigmoid(_dot(xb, w_ref[:, O_GA:IN_COLS]))


def _in_proj(x3, ln_g, ln_b, w_in_bf, *, tm, ln_input):
    G, T, _ = x3.shape
    row = lambda w: pl.BlockSpec((None, tm, w), lambda g, i: (g, i, 0))
    out_shape, out_specs = [], []
    if ln_input:
        out_shape.append(jax.ShapeDtypeStruct((G, T, D_MODEL), F32))
        out_specs.append(row(D_MODEL))
    for _ in range(4):
        out_shape.append(jax.ShapeDtypeStruct((G, T, ATTN_WIDTH), F32))
        out_specs.append(row(ATTN_WIDTH))
    for _ in range(2):
        out_shape.append(jax.ShapeDtypeStruct((G, T, D_MODEL), F32))
        out_specs.append(row(D_MODEL))
    return pl.pallas_call(
        functools.partial(_in_proj_kernel, ln_input=ln_input),
        grid=(G, T // tm),
        in_specs=[row(D_MODEL), _const_spec((1, D_MODEL)), _const_spec((1, D_MODEL)),
                  _const_spec((D_MODEL, IN_COLS))],
        out_specs=out_specs,
        out_shape=out_shape,
        compiler_params=_vmem(56 * MIB),
        name="in_proj",
    )(x3, ln_g, ln_b, w_in_bf)


SCAN_COLS = 512


def _s5_kernel(u_ref, h0r_ref, h0i_ref, ar_ref, ai_ref, wb_ref, wc_ref, d_ref, wglu_ref, bglu_ref,
               z_ref, hr_out, hi_out, xr_ref, xi_ref, hr_s, hi_s, *, nb, tc):
    @pl.when(pl.program_id(0) == 0)
    def _():
        hr_s[...] = h0r_ref[...]
        hi_s[...] = h0i_ref[...]

    half_in = SSM_WIDTH // 2
    half_st = N_STATE // 2
    u = jnp.swapaxes(u_ref[...], 0, 1).reshape(tc * nb, SSM_WIDTH)
    ub = u.astype(BF16)
    for h in range(2):
        ubh = ub[:, h * half_in:(h + 1) * half_in]
        xr_ref[:, h * half_st:(h + 1) * half_st] = _dot(ubh, wb_ref[h, :, :half_st])
        xi_ref[:, h * half_st:(h + 1) * half_st] = _dot(ubh, wb_ref[h, :, half_st:])

    for c in range(N_STATE // SCAN_COLS):
        cols = slice(c * SCAN_COLS, (c + 1) * SCAN_COLS)
        ar = jnp.broadcast_to(ar_ref[:, cols], (SUBLANES, SCAN_COLS))
        ai = jnp.broadcast_to(ai_ref[:, cols], (SUBLANES, SCAN_COLS))

        def seq_group(gi, _):
            r0 = pl.multiple_of(gi * SUBLANES, SUBLANES)

            def step(t, carry):
                hr, hi = carry
                r = pl.multiple_of(t * nb + r0, SUBLANES)
                nhr = ar * hr - ai * hi + xr_ref[pl.ds(r, SUBLANES), cols]
                nhi = ar * hi + ai * hr + xi_ref[pl.ds(r, SUBLANES), cols]
                xr_ref[pl.ds(r, SUBLANES), cols] = nhr
                xi_ref[pl.ds(r, SUBLANES), cols] = nhi
                return nhr, nhi

            hr, hi = lax.fori_loop(0, tc, step,
                                   (hr_s[pl.ds(r0, SUBLANES), cols], hi_s[pl.ds(r0, SUBLANES), cols]),
                                   unroll=2)
            hr_s[pl.ds(r0, SUBLANES), cols] = hr
            hi_s[pl.ds(r0, SUBLANES), cols] = hi
            return 0

        lax.fori_loop(0, nb // SUBLANES, seq_group, 0)

    ys = []
    for h in range(2):
        st = slice(h * half_st, (h + 1) * half_st)
        y = (_dot(xr_ref[:, st].astype(BF16), wc_ref[h, :half_st, :])
             + _dot(xi_ref[:, st].astype(BF16), wc_ref[h, half_st:, :]))
        ch = slice(h * half_in, (h + 1) * half_in)
        ys.append(y + u[:, ch] * d_ref[:, ch])
    y = jnp.concatenate(ys, axis=1)
    g = _gelu(y)
    z = g * jax.nn.sigmoid(_dot(g.astype(BF16), wglu_ref[...]) + bglu_ref[...])
    z_ref[...] = jnp.swapaxes(z.reshape(tc, nb, SSM_WIDTH), 0, 1)
    hr_out[...] = hr_s[...]
    hi_out[...] = hi_s[...]


def _s5(u, h0_re, h0_im, prm, *, tc):
    nb, T, _ = u.shape
    rows = tc * nb
    blk = pl.BlockSpec((nb, tc, SSM_WIDTH), lambda i: (0, i, 0))
    st = _const_spec((nb, N_STATE))
    return pl.pallas_call(
        functools.partial(_s5_kernel, nb=nb, tc=tc),
        grid=(T // tc,),
        in_specs=[blk, st, st, _const_spec((1, N_STATE)), _const_spec((1, N_STATE)),
                  _const_spec((2, SSM_WIDTH // 2, N_STATE)), _const_spec((2, N_STATE, SSM_WIDTH // 2)),
                  _const_spec((1, SSM_WIDTH)), _const_spec((SSM_WIDTH, SSM_WIDTH)),
                  _const_spec((1, SSM_WIDTH))],
        out_specs=[blk, pl.BlockSpec((nb, N_STATE), lambda i: (0, 0)),
                   pl.BlockSpec((nb, N_STATE), lambda i: (0, 0))],
        out_shape=[jax.ShapeDtypeStruct((nb, T, SSM_WIDTH), F32),
                   jax.ShapeDtypeStruct((nb, N_STATE), F32),
                   jax.ShapeDtypeStruct((nb, N_STATE), F32)],
        scratch_shapes=[pltpu.VMEM((rows, N_STATE), F32), pltpu.VMEM((rows, N_STATE), F32),
                        pltpu.VMEM((nb, N_STATE), F32), pltpu.VMEM((nb, N_STATE), F32)],
        compiler_params=pltpu.CompilerParams(vmem_limit_bytes=56 * MIB,
                                             dimension_semantics=("arbitrary",)),
        name="s5_scan",
    )(u, h0_re, h0_im, prm["abar_re"], prm["abar_im"], prm["wb"], prm["wc"], prm["d"],
      prm["w_glu"], prm["b_glu"])


def _s5_params(a_re, a_im, log_dt, b_re, b_im, c_re, c_im, d_skip, w_glu, b_glu):
    dt = jnp.exp(log_dt)[:, None]
    mag = jnp.exp(a_re * dt)
    abar_re, abar_im = mag * jnp.cos(a_im * dt), mag * jnp.sin(a_im * dt)
    nr, ni = abar_re - 1.0, abar_im
    den = a_re * a_re + a_im * a_im
    s_re = (nr * a_re + ni * a_im) / den
    s_im = (ni * a_re - nr * a_im) / den
    bb_re = s_re[..., None] * b_re - s_im[..., None] * b_im
    bb_im = s_re[..., None] * b_im + s_im[..., None] * b_re
    eye = jnp.eye(SSM_GROUPS, dtype=F32)
    wb_re = jnp.einsum('gpk,gh->gkhp', bb_re, eye).reshape(SSM_WIDTH, N_STATE)
    wb_im = jnp.einsum('gpk,gh->gkhp', bb_im, eye).reshape(SSM_WIDTH, N_STATE)
    wc_re = jnp.einsum('gkp,gh->gphk', c_re, eye).reshape(N_STATE, SSM_WIDTH)
    wc_im = jnp.einsum('gkp,gh->gphk', c_im, eye).reshape(N_STATE, SSM_WIDTH)
    hi_, hs = SSM_WIDTH // 2, N_STATE // 2
    wb = jnp.stack([jnp.concatenate([wb_re[h * hi_:(h + 1) * hi_, h * hs:(h + 1) * hs],
                                     wb_im[h * hi_:(h + 1) * hi_, h * hs:(h + 1) * hs]], axis=1)
                    for h in range(2)])
    wc = jnp.stack([jnp.concatenate([wc_re[h * hs:(h + 1) * hs, h * hi_:(h + 1) * hi_],
                                     -wc_im[h * hs:(h + 1) * hs, h * hi_:(h + 1) * hi_]], axis=0)
                    for h in range(2)])
    return {"abar_re": abar_re.reshape(1, N_STATE), "abar_im": abar_im.reshape(1, N_STATE),
            "wb": wb.astype(BF16), "wc": wc.astype(BF16), "d": d_skip.reshape(1, SSM_WIDTH),
            "w_glu": w_glu.astype(BF16), "b_glu": b_glu.reshape(1, SSM_WIDTH)}


def _from_key_on_matrix(n):
    m = lax.broadcasted_iota(jnp.int32, (n, n), 0)
    j = lax.broadcasted_iota(jnp.int32, (n, n), 1)
    return jnp.where(m >= j, 1.0, 0.0).astype(BF16)


def _softplus(z, mask):
    p = jnp.maximum(z, 0.0) + jnp.log(1.0 + jnp.exp(-jnp.abs(z)))
    if mask is not None:
        p = jnp.where(mask, p, 0.0)
    return p.astype(BF16)


def _stick_weights(z, sums, carry, mask):
    w = jnp.exp(z - (sums + carry))
    if mask is not None:
        w = jnp.where(mask, w, 0.0)
    return w.astype(BF16)


def _attn_prompt_kernel(bias_ref, q_ref, k_ref, v_ref, o_ref, kb_ref, vb_ref, from_ref, qm_ref, z_ref, p_ref,
                        w_ref, car_ref, acc_ref, *, seq_len):
    t_pad = kb_ref.shape[0]
    for src, dst in ((k_ref, kb_ref), (v_ref, vb_ref)):
        dst[0:seq_len, :] = src[...].astype(BF16)
        dst[seq_len:t_pad, :] = jnp.zeros((t_pad - seq_len, ATTN_WIDTH), BF16)
    from_ref[...] = _from_key_on_matrix(K_BLOCK)
    lane_head, _ = _iota_div((1, LANES), 1, HEAD_DIM)
    n_chunks = ATTN_WIDTH // LANES
    chunk = lambda c: slice(c * LANES, (c + 1) * LANES)
    bias = [bias_ref[h] for h in range(N_HEADS)]

    def q_block(r0, tq, k_diag, n_full):
        qi = lax.broadcasted_iota(jnp.int32, (tq, K_BLOCK), 0) + (r0 - k_diag)
        kj = lax.broadcasted_iota(jnp.int32, (tq, K_BLOCK), 1)
        diag_mask = kj < qi
        rows = lambda h: slice(h * Q_BLOCK, h * Q_BLOCK + tq)
        for c in range(n_chunks):
            qc = q_ref[pl.ds(r0, tq), chunk(c)]
            for e in range(2):
                qm_ref[2 * c + e, 0:tq, :] = jnp.where(lane_head == e, qc, 0.0).astype(BF16)

        def sweep(k0, mask, first):
            for h in range(N_HEADS):
                z = _dot_nt(qm_ref[h, 0:tq, :], kb_ref[pl.ds(k0, K_BLOCK), chunk(h // 2)]) + bias[h]
                z_ref[h, 0:tq, :] = z
                p_ref[rows(h), :] = _softplus(z, mask)
            if tq == Q_BLOCK:
                sums = _dot(p_ref[...], from_ref[...])
            for h in range(N_HEADS):
                sums_h = sums[rows(h)] if tq == Q_BLOCK else _dot(p_ref[rows(h), :], from_ref[...])
                carry = jnp.zeros((tq, 1), F32) if first else car_ref[h, 0:tq, :]
                w_ref[h, 0:tq, :] = _stick_weights(z_ref[h, 0:tq, :], sums_h, carry, mask)
                car_ref[h, 0:tq, :] = carry + sums_h[:, 0:1]
            for h in range(N_HEADS):
                pv = _dot(w_ref[h, 0:tq, :], vb_ref[pl.ds(k0, K_BLOCK), chunk(h // 2)])
                acc_ref[h, 0:tq, :] = pv if first else acc_ref[h, 0:tq, :] + pv

        sweep(k_diag, diag_mask, True)

        def full(jj, _):
            sweep(pl.multiple_of(k_diag - (jj + 1) * K_BLOCK, K_BLOCK), None, False)
            return 0

        lax.fori_loop(0, n_full, full, 0)
        for c in range(n_chunks):
            o_ref[pl.ds(r0, tq), chunk(c)] = jnp.where(
                lane_head == 0, acc_ref[2 * c, 0:tq, :], acc_ref[2 * c + 1, 0:tq, :])

    ratio = K_BLOCK // Q_BLOCK
    n_q = seq_len // Q_BLOCK

    def whole(i, _):
        jd = lax.shift_right_logical(i, ratio.bit_length() - 1)
        q_block(pl.multiple_of(i * Q_BLOCK, Q_BLOCK), Q_BLOCK, pl.multiple_of(jd * K_BLOCK, K_BLOCK), jd)
        return 0

    lax.fori_loop(0, n_q, whole, 0)
    if seq_len > n_q * Q_BLOCK:
        r0 = n_q * Q_BLOCK
        q_block(r0, seq_len - r0, r0 // K_BLOCK * K_BLOCK, r0 // K_BLOCK)


def _attn_prompt(q, k, v, bias):
    B, T, _ = q.shape
    assert K_BLOCK % Q_BLOCK == 0 and (K_BLOCK // Q_BLOCK) & (K_BLOCK // Q_BLOCK - 1) == 0
    assert (T % Q_BLOCK) % BF16_ROWS == 0
    t_pad = -(-T // K_BLOCK) * K_BLOCK
    blk = pl.BlockSpec((None, T, ATTN_WIDTH), lambda b: (b, 0, 0))
    return pl.pallas_call(
        functools.partial(_attn_prompt_kernel, seq_len=T),
        grid=(B,),
        in_specs=[pl.BlockSpec(memory_space=pltpu.SMEM), blk, blk, blk],
        out_specs=blk,
        out_shape=jax.ShapeDtypeStruct((B, T, ATTN_WIDTH), F32),
        scratch_shapes=[pltpu.VMEM((t_pad, ATTN_WIDTH), BF16), pltpu.VMEM((t_pad, ATTN_WIDTH), BF16),
                        pltpu.VMEM((K_BLOCK, K_BLOCK), BF16),
                        pltpu.VMEM((N_HEADS, Q_BLOCK, LANES), BF16),
                        pltpu.VMEM((N_HEADS, Q_BLOCK, K_BLOCK), F32),
                        pltpu.VMEM((N_HEADS * Q_BLOCK, K_BLOCK), BF16),
                        pltpu.VMEM((N_HEADS, Q_BLOCK, K_BLOCK), BF16),
                        pltpu.VMEM((N_HEADS, Q_BLOCK, 1), F32),
                        pltpu.VMEM((N_HEADS, Q_BLOCK, LANES), F32)],
        compiler_params=_vmem(56 * MIB),
        name="attn_prompt",
    )(bias, q, k, v)


def _attn_sample_kernel(pt_ref, bias_ref, q_ref, kn_ref, vn_ref, *rest, t_new, page, n_pages, n_seq):
    k_refs, v_refs = rest[:n_seq * n_pages], rest[n_seq * n_pages:2 * n_seq * n_pages]
    o_ref, pad_k, pad_v = rest[2 * n_seq * n_pages:]
    rows = N_HEADS * t_new
    from_on = _from_key_on_matrix(page)
    bias = bias_ref[...]
    row_head, _ = _iota_div((rows, ATTN_WIDTH), 0, t_new)
    lane_head, _ = _iota_div((rows, ATTN_WIDTH), 1, HEAD_DIM)
    own = row_head == lane_head
    _, tok = _iota_div((rows, page), 0, t_new)
    new_mask = lax.broadcasted_iota(jnp.int32, (rows, page), 1) < tok
    flat = lambda ref: ref[...].reshape(ATTN_WIDTH, page).astype(BF16)

    def scores(s, mask):
        z = s + bias
        return z, _dot(_softplus(z, mask), from_on), mask

    for j in range(n_seq):
        qx = jnp.where(own, jnp.concatenate([q_ref[j]] * N_HEADS, axis=0), 0.0).astype(BF16)
        for pad, new in ((pad_k, kn_ref), (pad_v, vn_ref)):
            pad[j] = jnp.zeros((page, ATTN_WIDTH), F32)
            pad[j, 0:t_new, :] = new[j]
        pages_k = k_refs[j * n_pages:(j + 1) * n_pages]
        pages_v = v_refs[j * n_pages:(j + 1) * n_pages]
        scored = [scores(_dot_nt(qx, pad_k[j].astype(BF16)), new_mask)]
        scored += [scores(_dot(qx, flat(pages_k[a])), None) for a in range(n_pages)]
        carry = jnp.zeros((rows, 1), F32)
        acc = jnp.zeros((rows, ATTN_WIDTH), F32)
        for i, (z, sums, mask) in enumerate(scored):
            w = _stick_weights(z, sums, carry, mask)
            carry = carry + sums[:, 0:1]
            acc = acc + (_dot(w, pad_v[j].astype(BF16)) if i == 0 else _dot_nt(w, flat(pages_v[i - 1])))
        acc = jnp.where(own, acc, 0.0)
        out = acc[0:t_new]
        for h in range(1, N_HEADS):
            out = out + acc[h * t_new:(h + 1) * t_new]
        o_ref[j] = out


def _attn_sample(q, k_new, v_new, cache_k, cache_v, page_table, bias_rows, layer):
    S, t_new, _ = q.shape
    n_pages = page_table.shape[1]
    page = cache_k.shape[4]
    rows = N_HEADS * t_new
    n_seq = SAMPLE_SEQS_PER_STEP
    assert S % n_seq == 0 and (t_new % BF16_ROWS == 0 or t_new == SUBLANES)
    tok = pl.BlockSpec((n_seq, t_new, ATTN_WIDTH), lambda s, pt: (s, 0, 0))

    def page_spec(j, a):
        return pl.BlockSpec((None, None, N_HEADS, HEAD_DIM, page),
                            lambda s, pt: (layer, pt[s * n_seq + j, n_pages - 1 - a], 0, 0, 0))

    n_blocks = n_seq * n_pages
    page_bytes = N_HEADS * HEAD_DIM * page * 4
    grid_spec = pltpu.PrefetchScalarGridSpec(
        num_scalar_prefetch=1,
        grid=(S // n_seq,),
        in_specs=[pl.BlockSpec((rows, 1), lambda s, pt: (0, 0)), tok, tok, tok]
        + [page_spec(j, a) for j in range(n_seq) for a in range(n_pages)] * 2,
        out_specs=tok,
        scratch_shapes=[pltpu.VMEM((n_seq, page, ATTN_WIDTH), F32), pltpu.VMEM((n_seq, page, ATTN_WIDTH), F32)],
    )
    return pl.pallas_call(
        functools.partial(_attn_sample_kernel, t_new=t_new, page=page, n_pages=n_pages, n_seq=n_seq),
        grid_spec=grid_spec,
        out_shape=jax.ShapeDtypeStruct((S, t_new, ATTN_WIDTH), F32),
        compiler_params=pltpu.CompilerParams(vmem_limit_bytes=4 * n_blocks * page_bytes + 16 * MIB,
                                             dimension_semantics=("arbitrary",)),
        name="attn_sample",
    )(page_table, bias_rows, q, k_new, v_new, *([cache_k] * n_blocks), *([cache_v] * n_blocks))


def _mix_kernel(x_ref, gs_ref, ga_ref, z_ref, o_ref, wbs_ref, wba_ref, wout_ref, g_ref, b_ref, y_ref, *, alpha):
    merged = (gs_ref[...] * _dot(z_ref[...].astype(BF16), wbs_ref[...])
              + ga_ref[...] * _dot(o_ref[...].astype(BF16), wba_ref[...]))
    mix = _dot(merged.astype(BF16), wout_ref[...])
    y_ref[...] = _layer_norm(alpha * x_ref[...] + mix, g_ref[...], b_ref[...])


def _mix(x3, gs, ga, z, o, wbs, wba, wout, ln_g, ln_b, *, tm, alpha):
    G, T, _ = x3.shape
    row = lambda w: pl.BlockSpec((None, tm, w), lambda g, i: (g, i, 0))
    return pl.pallas_call(
        functools.partial(_mix_kernel, alpha=alpha),
        grid=(G, T // tm),
        in_specs=[row(D_MODEL), row(D_MODEL), row(D_MODEL),
                  row(SSM_WIDTH), row(ATTN_WIDTH),
                  _const_spec((SSM_WIDTH, D_MODEL)), _const_spec((ATTN_WIDTH, D_MODEL)),
                  _const_spec((D_MODEL, D_MODEL)), _const_spec((1, D_MODEL)), _const_spec((1, D_MODEL))],
        out_specs=row(D_MODEL),
        out_shape=jax.ShapeDtypeStruct((G, T, D_MODEL), F32),
        compiler_params=_vmem(48 * MIB),
        name="mix",
    )(x3, gs, ga, z, o, wbs, wba, wout, ln_g, ln_b)


def _chunk_major(a):
    *lead, rows, cols = a.shape
    return jnp.swapaxes(a.reshape(*lead, rows, cols // FF_CHUNK, FF_CHUNK), -3, -2)


def _from_chunk_major(a):
    *lead, n, rows, _ = a.shape
    return jnp.swapaxes(a, -3, -2).reshape(*lead, rows, n * FF_CHUNK)


def _ffn_kernel(x_ref, prev_ref, wup_ref, cw_ref, cb_ref, wdown_ref, g_ref, b_ref,
                y_ref, up_ref, acc_ref, xb_ref, ua_ref, ub_ref, *, alpha, seq_len, carry_rows):
    tm = x_ref.shape[0]
    xb_ref[...] = x_ref[...].astype(BF16)
    if seq_len == tm:
        t_loc = lax.broadcasted_iota(jnp.int32, (tm, 1), 0)
    else:
        _, t_loc = _iota_div((tm, 1), 0, seq_len)
    if carry_rows:
        @pl.when(pl.program_id(1) == 0)
        def _():
            up_ref[...] = prev_ref[...]

    def project(c, u_ref):
        u_ref[0] = _dot(xb_ref[...], wup_ref[c])
        u_ref[1] = _dot(xb_ref[...], wup_ref[c + N_FF_CHUNKS])

    def conv(j, up):
        if carry_rows:
            before2 = jnp.broadcast_to(up_ref[j, SUBLANES - 2:SUBLANES - 1, :], (SUBLANES, FF_CHUNK))
            before1 = jnp.broadcast_to(up_ref[j, SUBLANES - 1:SUBLANES, :], (SUBLANES, FF_CHUNK))
            t8 = t_loc[0:SUBLANES]
            up1, up2 = pltpu.roll(up, 1, axis=0), pltpu.roll(up, 2, axis=0)
            head1 = jnp.where(t8 >= 1, up1[0:SUBLANES], before1)
            head2 = jnp.where(t8 >= 2, up2[0:SUBLANES], jnp.where(t8 == 0, before2, before1))
            up1 = jnp.concatenate([head1, up1[SUBLANES:]], axis=0)
            up2 = jnp.concatenate([head2, up2[SUBLANES:]], axis=0)
            up_ref[j] = up[tm - SUBLANES:tm, :]
        else:
            n_seqs = tm // seq_len
            per_row = lambda s: jnp.broadcast_to(s[:, None, :], (n_seqs, seq_len, FF_CHUNK)).reshape(tm, FF_CHUNK)
            before2, before1 = per_row(prev_ref[0, j]), per_row(prev_ref[1, j])
            up1 = jnp.where(t_loc >= 1, pltpu.roll(up, 1, axis=0), before1)
            up2 = jnp.where(t_loc >= 2, pltpu.roll(up, 2, axis=0), jnp.where(t_loc == 0, before2, before1))
            by_seq = up.reshape(n_seqs, seq_len, FF_CHUNK)
            up_ref[0, j] = by_seq[:, seq_len - 2, :]
            up_ref[1, j] = by_seq[:, seq_len - 1, :]
        cw = cw_ref[j]
        return cb_ref[j] + cw[0:1] * up2 + cw[1:2] * up1 + cw[2:3] * up

    def down(c, u_ref):
        h = _gelu(conv(c, u_ref[0])) * conv(c + N_FF_CHUNKS, u_ref[1])
        return _dot(h.astype(BF16), wdown_ref[c])

    assert N_FF_CHUNKS % 2 == 1
    project(0, ua_ref)
    acc_ref[...] = jnp.zeros_like(acc_ref)

    def pair(i, _):
        project(2 * i + 1, ub_ref)
        acc_ref[...] += down(2 * i, ua_ref)
        project(2 * i + 2, ua_ref)
        acc_ref[...] += down(2 * i + 1, ub_ref)
        return 0

    lax.fori_loop(0, N_FF_CHUNKS // 2, pair, 0)
    last = down(N_FF_CHUNKS - 1, ua_ref)
    y_ref[...] = _layer_norm(alpha * x_ref[...] + (acc_ref[...] + last), g_ref[...], b_ref[...])


def _ffn(x3, prev, wup, conv_w, conv_b, wdown, ln_g, ln_b, *, tm, alpha, seq_len, carry_rows):
    G, T, _ = x3.shape
    n2 = 2 * N_FF_CHUNKS
    row = lambda w: pl.BlockSpec((None, tm, w), lambda g, i: (g, i, 0))
    if carry_rows:
        up_spec = pl.BlockSpec((None, n2, SUBLANES, FF_CHUNK), lambda g, i: (g, 0, 0, 0))
        up_shape = jax.ShapeDtypeStruct((G, n2, SUBLANES, FF_CHUNK), F32)
    else:
        up_spec = pl.BlockSpec((None, CONV_W - 1, n2, tm // seq_len, FF_CHUNK), lambda g, i: (g, 0, 0, i, 0))
        up_shape = jax.ShapeDtypeStruct((G, CONV_W - 1, n2, T // seq_len, FF_CHUNK), F32)
    return pl.pallas_call(
        functools.partial(_ffn_kernel, alpha=alpha, seq_len=seq_len, carry_rows=carry_rows),
        grid=(G, T // tm),
        in_specs=[row(D_MODEL), up_spec, _const_spec((n2, D_MODEL, FF_CHUNK)),
                  _const_spec((n2, CONV_W, FF_CHUNK)), _const_spec((n2, 1, FF_CHUNK)),
                  _const_spec((N_FF_CHUNKS, FF_CHUNK, D_MODEL)), _const_spec((1, D_MODEL)),
                  _const_spec((1, D_MODEL))],
        out_specs=[row(D_MODEL), up_spec],
        out_shape=[jax.ShapeDtypeStruct((G, T, D_MODEL), F32), up_shape],
        scratch_shapes=[pltpu.VMEM((tm, D_MODEL), F32), pltpu.VMEM((tm, D_MODEL), BF16),
                        pltpu.VMEM((2, tm, FF_CHUNK), F32), pltpu.VMEM((2, tm, FF_CHUNK), F32)],
        compiler_params=pltpu.CompilerParams(vmem_limit_bytes=56 * MIB,
                                             dimension_semantics=("arbitrary", "arbitrary")),
        name="conv_ffn",
    )(x3, prev, wup, conv_w, conv_b, wdown, ln_g, ln_b)


def _largest_divisor(n, cap, multiple):
    best = None
    for d in range(multiple, cap + 1, multiple):
        if n % d == 0:
            best = d
    assert best is not None, (n, cap, multiple)
    return best


def _layer_weights(l, w_in, ssm, w_branch_ssm, w_branch_attn, w_out, ln1_g, ln1_b, w_up, conv_w, conv_b,
                   w_down, ln2_g, ln2_b, sb_bias):
    row = lambda a: a[l].reshape(1, -1)
    return {
        "w_in": w_in[l].astype(BF16), "s5": _s5_params(*[a[l] for a in ssm]),
        "wbs": w_branch_ssm[l].astype(BF16), "wba": w_branch_attn[l].astype(BF16),
        "wout": w_out[l].astype(BF16), "ln1_g": row(ln1_g), "ln1_b": row(ln1_b),
        "wup": _chunk_major(w_up[l].astype(BF16)), "conv_w": _chunk_major(conv_w[l]),
        "conv_b": _chunk_major(row(conv_b)),
        "wdown": w_down[l].astype(BF16).reshape(N_FF_CHUNKS, FF_CHUNK, D_MODEL),
        "ln2_g": row(ln2_g), "ln2_b": row(ln2_b), "sb_bias": sb_bias[l],
    }


def kernel(x_prompt, x_sample, cache_k, cache_v, state_ssm_re, state_ssm_im, state_conv, page_table, meta_tokens, ln_in_g, ln_in_b, w_in, ssm_a_re, ssm_a_im, ssm_log_dt, ssm_b_re, ssm_b_im, ssm_c_re, ssm_c_im, ssm_d, w_glu, b_glu, w_branch_ssm, w_branch_attn, w_out, ln1_g, ln1_b, w_up, conv_w, conv_b, w_down, ln2_g, ln2_b, sb_bias):
    B, seq, _ = x_prompt.shape
    S, t_new, _ = x_sample.shape
    depth = w_in.shape[0]
    alpha = (2.0 * depth) ** 0.25
    T = N_META + seq
    assert B % SUBLANES == 0 and S % SUBLANES == 0 and seq % Q_BLOCK == 0 and t_new >= CONV_W - 1
    assert cache_k.shape[3:] == (N_HEADS, HEAD_DIM)
    ck = jnp.transpose(cache_k, (0, 1, 3, 4, 2))
    cv = jnp.transpose(cache_v, (0, 1, 3, 4, 2))

    ssm = (ssm_a_re, ssm_a_im, ssm_log_dt, ssm_b_re, ssm_b_im, ssm_c_re, ssm_c_im, ssm_d, w_glu, b_glu)
    lng, lnb = ln_in_g.reshape(1, -1), ln_in_b.reshape(1, -1)

    tm_p = _largest_divisor(T, PROMPT_ROWS_CAP, BF16_ROWS)
    tc_p = _largest_divisor(T, SCAN_STEPS_CAP, SUBLANES)
    rows_s = S * t_new
    tm_s = _largest_divisor(rows_s, SAMPLE_ROWS_CAP, BF16_ROWS * t_new)

    meta = jnp.broadcast_to(meta_tokens[None], (B, N_META, D_MODEL))
    xp = jnp.concatenate([meta, x_prompt], axis=1)
    xs = x_sample.reshape(1, rows_s, D_MODEL)
    zeros_state = jnp.zeros((B, N_STATE), F32)
    zeros_conv = jnp.zeros((B, 2 * N_FF_CHUNKS, SUBLANES, FF_CHUNK), F32)

    outs_p, outs_s = [], []
    for l in range(depth):
        w = _layer_weights(l, w_in, ssm, w_branch_ssm, w_branch_attn, w_out, ln1_g, ln1_b, w_up, conv_w,
                           conv_b, w_down, ln2_g, ln2_b, sb_bias)
        first = l == 0

        res = _in_proj(xp, lng, lnb, w["w_in"], tm=tm_p, ln_input=first)
        if first:
            xp, *res = res
        u, q, k, v, gs, ga = res
        z, hr, hi = _s5(u, zeros_state, zeros_state, w["s5"], tc=tc_p)
        o = _attn_prompt(q, k, v, w["sb_bias"])
        x1 = _mix(xp, gs, ga, z, o, w["wbs"], w["wba"], w["wout"],
                  w["ln1_g"], w["ln1_b"], tm=tm_p, alpha=alpha)
        xp, tail = _ffn(x1, zeros_conv, w["wup"], w["conv_w"], w["conv_b"], w["wdown"], w["ln2_g"],
                        w["ln2_b"], tm=tm_p, alpha=alpha, seq_len=tm_p, carry_rows=True)
        outs_p.append((k.reshape(B, T, N_HEADS, HEAD_DIM), v.reshape(B, T, N_HEADS, HEAD_DIM),
                       hr.reshape(B, SSM_GROUPS, SSM_STATE), hi.reshape(B, SSM_GROUPS, SSM_STATE),
                       _from_chunk_major(tail)[:, SUBLANES - (CONV_W - 1):]))

        res = _in_proj(xs, lng, lnb, w["w_in"], tm=tm_s, ln_input=first)
        if first:
            xs, *res = res
        u, q, k, v, gs, ga = res
        z, hr, hi = _s5(u.reshape(S, t_new, SSM_WIDTH), state_ssm_re[l].reshape(S, N_STATE),
                        state_ssm_im[l].reshape(S, N_STATE), w["s5"], tc=t_new)
        k3, v3 = k.reshape(S, t_new, ATTN_WIDTH), v.reshape(S, t_new, ATTN_WIDTH)
        bias_rows = jnp.repeat(w["sb_bias"], t_new).reshape(N_HEADS * t_new, 1)
        o = _attn_sample(q.reshape(S, t_new, ATTN_WIDTH), k3, v3, ck, cv, page_table, bias_rows, l)
        x1 = _mix(xs, gs, ga, z.reshape(1, rows_s, SSM_WIDTH), o.reshape(1, rows_s, ATTN_WIDTH), w["wbs"],
                  w["wba"], w["wout"],
                  w["ln1_g"], w["ln1_b"], tm=tm_s, alpha=alpha)
        prev = state_conv[l].reshape(S, CONV_W - 1, 2 * N_FF_CHUNKS, FF_CHUNK).transpose(1, 2, 0, 3)[None]
        xs, up = _ffn(x1, prev, w["wup"], w["conv_w"], w["conv_b"], w["wdown"], w["ln2_g"],
                      w["ln2_b"], tm=tm_s, alpha=alpha, seq_len=t_new, carry_rows=False)
        outs_s.append((k3.reshape(S, t_new, N_HEADS, HEAD_DIM), v3.reshape(S, t_new, N_HEADS, HEAD_DIM),
                       hr.reshape(S, SSM_GROUPS, SSM_STATE), hi.reshape(S, SSM_GROUPS, SSM_STATE),
                       up[0].transpose(2, 0, 1, 3).reshape(S, CONV_W - 1, 2 * D_FF)))

    stack = lambda outs, j: jnp.stack([o[j] for o in outs])
    return (xp[:, N_META:], xs.reshape(S, t_new, D_MODEL),
            *[stack(outs_p, j) for j in range(5)], *[stack(outs_s, j) for j in range(5)])
```

```python
import functools
import math

import jax
import jax.numpy as jnp
from jax import lax
from jax.experimental import pallas as pl
from jax.experimental.pallas import tpu as pltpu

F32 = jnp.float32
BF16 = jnp.bfloat16

D_MODEL = 1024
N_META = 16
SSM_WIDTH = 512
SSM_GROUP = 16
SSM_GROUPS = 32
SSM_STATE = 64
N_STATE = SSM_GROUPS * SSM_STATE
HEAD_DIM = 64
N_HEADS = 8
ATTN_WIDTH = 512
D_FF = 2816
CONV_W = 3
LN_EPS = 1e-5
O_Q = SSM_WIDTH
O_K = O_Q + ATTN_WIDTH
O_V = O_K + ATTN_WIDTH
O_GS = O_V + ATTN_WIDTH
O_GA = O_GS + D_MODEL
IN_COLS = O_GA + D_MODEL

SUBLANES = 8
LANES = 128
BF16_ROWS = 16
MXU_COLS = 256
Q_BLOCK = 256
K_BLOCK = 256
FF_CHUNK = MXU_COLS
N_FF_CHUNKS = D_FF // FF_CHUNK
MIB = 1024 * 1024
PROMPT_ROWS_CAP = 700
SAMPLE_ROWS_CAP = 256
SCAN_STEPS_CAP = 96
FFN_PAIR_UNROLL = N_FF_CHUNKS // 2
SAMPLE_SEQS_PER_STEP = 1


def _vmem(nbytes):
    return pltpu.CompilerParams(vmem_limit_bytes=int(nbytes))


def _const_spec(shape):
    nd = len(shape)
    return pl.BlockSpec(shape, lambda *_: (0,) * nd, pipeline_mode=pl.Buffered(1))


def _layer_norm(x, g, b):
    mu = jnp.mean(x, axis=-1, keepdims=True)
    xc = x - mu
    var = jnp.mean(xc * xc, axis=-1, keepdims=True)
    return xc * lax.rsqrt(var + LN_EPS) * g + b


def _gelu(x):
    return 0.5 * x * (1.0 + lax.erf(x * math.sqrt(0.5)))


def _iota_div(shape, axis, n):
    assert n & (n - 1) == 0, n
    i = lax.broadcasted_iota(jnp.int32, shape, axis)
    return lax.shift_right_logical(i, n.bit_length() - 1), lax.bitwise_and(i, n - 1)


def _dot(a, b):
    return jnp.dot(a, b, preferred_element_type=F32)


def _dot_nt(a, b):
    return lax.dot_general(a, b, (((1,), (1,)), ((), ())), preferred_element_type=F32)


def _in_proj_kernel(x_ref, g_ref, b_ref, w_ref, *out_refs, ln_input):
    if ln_input:
        xn_ref, u_ref, q_ref, k_ref, v_ref, gs_ref, ga_ref = out_refs
    else:
        u_ref, q_ref, k_ref, v_ref, gs_ref, ga_ref = out_refs
    x = x_ref[...]
    if ln_input:
        x = _layer_norm(x, g_ref[...], b_ref[...])
        xn_ref[...] = x
    xb = x.astype(BF16)
    u_ref[...] = _dot(xb, w_ref[:, 0:O_Q])
    q_ref[...] = _dot(xb, w_ref[:, O_Q:O_K]) * (HEAD_DIM ** -0.5)
    k_ref[...] = _dot(xb, w_ref[:, O_K:O_V])
    v_ref[...] = _dot(xb, w_ref[:, O_V:O_GS])
    gs_ref[...] = jax.nn.sigmoid(_dot(xb, w_ref[:, O_GS:O_GA])).astype(gs_ref.dtype)
    ga_ref[...] = jax.nn.sigmoid(_dot(xb, w_ref[:, O_GA:IN_COLS])).astype(ga_ref.dtype)


def _in_proj(x3, ln_g, ln_b, w_in_bf, *, tm, ln_input):
    G, T, _ = x3.shape
    row = lambda w: pl.BlockSpec((None, tm, w), lambda g, i: (g, i, 0))
    out_shape, out_specs = [], []
    if ln_input:
        out_shape.append(jax.ShapeDtypeStruct((G, T, D_MODEL), F32))
        out_specs.append(row(D_MODEL))
    for _ in range(4):
        out_shape.append(jax.ShapeDtypeStruct((G, T, ATTN_WIDTH), F32))
        out_specs.append(row(ATTN_WIDTH))
    for _ in range(2):
        out_shape.append(jax.ShapeDtypeStruct((G, T, D_MODEL), BF16))
        out_specs.append(row(D_MODEL))
    return pl.pallas_call(
        functools.partial(_in_proj_kernel, ln_input=ln_input),
        grid=(G, T // tm),
        in_specs=[row(D_MODEL), _const_spec((1, D_MODEL)), _const_spec((1, D_MODEL)),
                  _const_spec((D_MODEL, IN_COLS))],
        out_specs=out_specs,
        out_shape=out_shape,
        compiler_params=_vmem(56 * MIB),
        name="in_proj",
    )(x3, ln_g, ln_b, w_in_bf)


SCAN_COLS = 512


def _s5_kernel(u_ref, h0r_ref, h0i_ref, ar_ref, ai_ref, wb_ref, wc_ref, d_ref, wglu_ref, bglu_ref,
               z_ref, hr_out, hi_out, xr_ref, xi_ref, hr_s, hi_s, *, nb, tc):
    @pl.when(pl.program_id(0) == 0)
    def _():
        hr_s[...] = h0r_ref[...]
        hi_s[...] = h0i_ref[...]

    half_in = SSM_WIDTH // 2
    half_st = N_STATE // 2
    u = jnp.swapaxes(u_ref[...], 0, 1).reshape(tc * nb, SSM_WIDTH)
    ub = u.astype(BF16)
    for h in range(2):
        ubh = ub[:, h * half_in:(h + 1) * half_in]
        xr_ref[:, h * half_st:(h + 1) * half_st] = _dot(ubh, wb_ref[h, :, :half_st])
        xi_ref[:, h * half_st:(h + 1) * half_st] = _dot(ubh, wb_ref[h, :, half_st:])

    for c in range(N_STATE // SCAN_COLS):
        cols = slice(c * SCAN_COLS, (c + 1) * SCAN_COLS)
        ar = jnp.broadcast_to(ar_ref[:, cols], (SUBLANES, SCAN_COLS))
        ai = jnp.broadcast_to(ai_ref[:, cols], (SUBLANES, SCAN_COLS))

        def seq_group(gi, _):
            r0 = pl.multiple_of(gi * SUBLANES, SUBLANES)

            def step(t, carry):
                hr, hi = carry
                r = pl.multiple_of(t * nb + r0, SUBLANES)
                nhr = ar * hr - ai * hi + xr_ref[pl.ds(r, SUBLANES), cols]
                nhi = ar * hi + ai * hr + xi_ref[pl.ds(r, SUBLANES), cols]
                xr_ref[pl.ds(r, SUBLANES), cols] = nhr
                xi_ref[pl.ds(r, SUBLANES), cols] = nhi
                return nhr, nhi

            hr, hi = lax.fori_loop(0, tc, step,
                                   (hr_s[pl.ds(r0, SUBLANES), cols], hi_s[pl.ds(r0, SUBLANES), cols]),
                                   unroll=2)
            hr_s[pl.ds(r0, SUBLANES), cols] = hr
            hi_s[pl.ds(r0, SUBLANES), cols] = hi
            return 0

        lax.fori_loop(0, nb // SUBLANES, seq_group, 0)

    ys = []
    for h in range(2):
        st = slice(h * half_st, (h + 1) * half_st)
        y = (_dot(xr_ref[:, st].astype(BF16), wc_ref[h, :half_st, :])
             + _dot(xi_ref[:, st].astype(BF16), wc_ref[h, half_st:, :]))
        ch = slice(h * half_in, (h + 1) * half_in)
        ys.append(y + u[:, ch] * d_ref[:, ch])
    y = jnp.concatenate(ys, axis=1)
    g = _gelu(y)
    z = g * jax.nn.sigmoid(_dot(g.astype(BF16), wglu_ref[...]) + bglu_ref[...])
    z_ref[...] = jnp.swapaxes(z.reshape(tc, nb, SSM_WIDTH), 0, 1).astype(z_ref.dtype)
    hr_out[...] = hr_s[...]
    hi_out[...] = hi_s[...]


def _s5(u, h0_re, h0_im, prm, *, tc):
    nb, T, _ = u.shape
    rows = tc * nb
    blk = pl.BlockSpec((nb, tc, SSM_WIDTH), lambda i: (0, i, 0))
    st = _const_spec((nb, N_STATE))
    return pl.pallas_call(
        functools.partial(_s5_kernel, nb=nb, tc=tc),
        grid=(T // tc,),
        in_specs=[blk, st, st, _const_spec((1, N_STATE)), _const_spec((1, N_STATE)),
                  _const_spec((2, SSM_WIDTH // 2, N_STATE)), _const_spec((2, N_STATE, SSM_WIDTH // 2)),
                  _const_spec((1, SSM_WIDTH)), _const_spec((SSM_WIDTH, SSM_WIDTH)),
                  _const_spec((1, SSM_WIDTH))],
        out_specs=[blk, pl.BlockSpec((nb, N_STATE), lambda i: (0, 0)),
                   pl.BlockSpec((nb, N_STATE), lambda i: (0, 0))],
        out_shape=[jax.ShapeDtypeStruct((nb, T, SSM_WIDTH), BF16 if tc % BF16_ROWS == 0 else F32),
                   jax.ShapeDtypeStruct((nb, N_STATE), F32),
                   jax.ShapeDtypeStruct((nb, N_STATE), F32)],
        scratch_shapes=[pltpu.VMEM((rows, N_STATE), F32), pltpu.VMEM((rows, N_STATE), F32),
                        pltpu.VMEM((nb, N_STATE), F32), pltpu.VMEM((nb, N_STATE), F32)],
        compiler_params=pltpu.CompilerParams(vmem_limit_bytes=56 * MIB,
                                             dimension_semantics=("arbitrary",)),
        name="s5_scan",
    )(u, h0_re, h0_im, prm["abar_re"], prm["abar_im"], prm["wb"], prm["wc"], prm["d"],
      prm["w_glu"], prm["b_glu"])


def _s5_params(a_re, a_im, log_dt, b_re, b_im, c_re, c_im, d_skip, w_glu, b_glu):
    dt = jnp.exp(log_dt)[:, None]
    mag = jnp.exp(a_re * dt)
    abar_re, abar_im = mag * jnp.cos(a_im * dt), mag * jnp.sin(a_im * dt)
    nr, ni = abar_re - 1.0, abar_im
    den = a_re * a_re + a_im * a_im
    s_re = (nr * a_re + ni * a_im) / den
    s_im = (ni * a_re - nr * a_im) / den
    bb_re = s_re[..., None] * b_re - s_im[..., None] * b_im
    bb_im = s_re[..., None] * b_im + s_im[..., None] * b_re
    eye = jnp.eye(SSM_GROUPS, dtype=F32)
    wb_re = jnp.einsum('gpk,gh->gkhp', bb_re, eye).reshape(SSM_WIDTH, N_STATE)
    wb_im = jnp.einsum('gpk,gh->gkhp', bb_im, eye).reshape(SSM_WIDTH, N_STATE)
    wc_re = jnp.einsum('gkp,gh->gphk', c_re, eye).reshape(N_STATE, SSM_WIDTH)
    wc_im = jnp.einsum('gkp,gh->gphk', c_im, eye).reshape(N_STATE, SSM_WIDTH)
    hi_, hs = SSM_WIDTH // 2, N_STATE // 2
    wb = jnp.stack([jnp.concatenate([wb_re[h * hi_:(h + 1) * hi_, h * hs:(h + 1) * hs],
                                     wb_im[h * hi_:(h + 1) * hi_, h * hs:(h + 1) * hs]], axis=1)
                    for h in range(2)])
    wc = jnp.stack([jnp.concatenate([wc_re[h * hs:(h + 1) * hs, h * hi_:(h + 1) * hi_],
                                     -wc_im[h * hs:(h + 1) * hs, h * hi_:(h + 1) * hi_]], axis=0)
                    for h in range(2)])
    return {"abar_re": abar_re.reshape(1, N_STATE), "abar_im": abar_im.reshape(1, N_STATE),
            "wb": wb.astype(BF16), "wc": wc.astype(BF16), "d": d_skip.reshape(1, SSM_WIDTH),
            "w_glu": w_glu.astype(BF16), "b_glu": b_glu.reshape(1, SSM_WIDTH)}


def _from_key_on_matrix(n):
    m = lax.broadcasted_iota(jnp.int32, (n, n), 0)
    j = lax.broadcasted_iota(jnp.int32, (n, n), 1)
    return jnp.where(m >= j, 1.0, 0.0).astype(BF16)


def _softplus(z, mask):
    p = jnp.maximum(z, 0.0) + jnp.log(1.0 + jnp.exp(-jnp.abs(z)))
    if mask is not None:
        p = jnp.where(mask, p, 0.0)
    return p.astype(BF16)


def _stick_weights(z, sums, carry, mask):
    w = jnp.exp(z - (sums + carry))
    if mask is not None:
        w = jnp.where(mask, w, 0.0)
    return w.astype(BF16)


def _attn_prompt_kernel(bias_ref, q_ref, k_ref, v_ref, o_ref, kb_ref, vb_ref, from_ref, qm_ref, z_ref, p_ref,
                        w_ref, car_ref, acc_ref, *, seq_len):
    t_pad = kb_ref.shape[0]
    for src, dst in ((k_ref, kb_ref), (v_ref, vb_ref)):
        dst[0:seq_len, :] = src[...].astype(BF16)
        dst[seq_len:t_pad, :] = jnp.zeros((t_pad - seq_len, ATTN_WIDTH), BF16)
    from_ref[...] = _from_key_on_matrix(K_BLOCK)
    lane_head, _ = _iota_div((1, LANES), 1, HEAD_DIM)
    n_chunks = ATTN_WIDTH // LANES
    chunk = lambda c: slice(c * LANES, (c + 1) * LANES)
    bias = [bias_ref[h] for h in range(N_HEADS)]

    def q_block(r0, tq, k_diag, n_full):
        qi = lax.broadcasted_iota(jnp.int32, (tq, K_BLOCK), 0) + (r0 - k_diag)
        kj = lax.broadcasted_iota(jnp.int32, (tq, K_BLOCK), 1)
        diag_mask = kj < qi
        rows = lambda h: slice(h * Q_BLOCK, h * Q_BLOCK + tq)
        for c in range(n_chunks):
            qc = q_ref[pl.ds(r0, tq), chunk(c)]
            for e in range(2):
                qm_ref[2 * c + e, 0:tq, :] = jnp.where(lane_head == e, qc, 0.0).astype(BF16)

        def sweep(k0, mask, first):
            for h in range(N_HEADS):
                z = _dot_nt(qm_ref[h, 0:tq, :], kb_ref[pl.ds(k0, K_BLOCK), chunk(h // 2)]) + bias[h]
                z_ref[h, 0:tq, :] = z
                p_ref[rows(h), :] = _softplus(z, mask)
            if tq == Q_BLOCK:
                sums = _dot(p_ref[...], from_ref[...])
            for h in range(N_HEADS):
                sums_h = sums[rows(h)] if tq == Q_BLOCK else _dot(p_ref[rows(h), :], from_ref[...])
                carry = jnp.zeros((tq, 1), F32) if first else car_ref[h, 0:tq, :]
                w_ref[h, 0:tq, :] = _stick_weights(z_ref[h, 0:tq, :], sums_h, carry, mask)
                car_ref[h, 0:tq, :] = carry + sums_h[:, 0:1]
            for h in range(N_HEADS):
                pv = _dot(w_ref[h, 0:tq, :], vb_ref[pl.ds(k0, K_BLOCK), chunk(h // 2)])
                acc_ref[h, 0:tq, :] = pv if first else acc_ref[h, 0:tq, :] + pv

        sweep(k_diag, diag_mask, True)

        def full(jj, _):
            sweep(pl.multiple_of(k_diag - (jj + 1) * K_BLOCK, K_BLOCK), None, False)
            return 0

        lax.fori_loop(0, n_full, full, 0)
        for c in range(n_chunks):
            o_ref[pl.ds(r0, tq), chunk(c)] = jnp.where(
                lane_head == 0, acc_ref[2 * c, 0:tq, :], acc_ref[2 * c + 1, 0:tq, :]).astype(o_ref.dtype)

    ratio = K_BLOCK // Q_BLOCK
    n_q = seq_len // Q_BLOCK

    def whole(i, _):
        jd = lax.shift_right_logical(i, ratio.bit_length() - 1)
        q_block(pl.multiple_of(i * Q_BLOCK, Q_BLOCK), Q_BLOCK, pl.multiple_of(jd * K_BLOCK, K_BLOCK), jd)
        return 0

    lax.fori_loop(0, n_q, whole, 0)
    if seq_len > n_q * Q_BLOCK:
        r0 = n_q * Q_BLOCK
        q_block(r0, seq_len - r0, r0 // K_BLOCK * K_BLOCK, r0 // K_BLOCK)


def _attn_prompt(q, k, v, bias):
    B, T, _ = q.shape
    assert K_BLOCK % Q_BLOCK == 0 and (K_BLOCK // Q_BLOCK) & (K_BLOCK // Q_BLOCK - 1) == 0
    assert (T % Q_BLOCK) % BF16_ROWS == 0
    t_pad = -(-T // K_BLOCK) * K_BLOCK
    blk = pl.BlockSpec((None, T, ATTN_WIDTH), lambda b: (b, 0, 0))
    return pl.pallas_call(
        functools.partial(_attn_prompt_kernel, seq_len=T),
        grid=(B,),
        in_specs=[pl.BlockSpec(memory_space=pltpu.SMEM), blk, blk, blk],
        out_specs=blk,
        out_shape=jax.ShapeDtypeStruct((B, T, ATTN_WIDTH), BF16),
        scratch_shapes=[pltpu.VMEM((t_pad, ATTN_WIDTH), BF16), pltpu.VMEM((t_pad, ATTN_WIDTH), BF16),
                        pltpu.VMEM((K_BLOCK, K_BLOCK), BF16),
                        pltpu.VMEM((N_HEADS, Q_BLOCK, LANES), BF16),
                        pltpu.VMEM((N_HEADS, Q_BLOCK, K_BLOCK), F32),
                        pltpu.VMEM((N_HEADS * Q_BLOCK, K_BLOCK), BF16),
                        pltpu.VMEM((N_HEADS, Q_BLOCK, K_BLOCK), BF16),
                        pltpu.VMEM((N_HEADS, Q_BLOCK, 1), F32),
                        pltpu.VMEM((N_HEADS, Q_BLOCK, LANES), F32)],
        compiler_params=_vmem(56 * MIB),
        name="attn_prompt",
    )(bias, q, k, v)


def _attn_sample_kernel(pt_ref, bias_ref, q_ref, kn_ref, vn_ref, *rest, t_new, page, n_pages, n_seq):
    k_refs, v_refs = rest[:n_seq * n_pages], rest[n_seq * n_pages:2 * n_seq * n_pages]
    o_ref, pad_k, pad_v = rest[2 * n_seq * n_pages:]
    rows = N_HEADS * t_new
    from_on = _from_key_on_matrix(page)
    bias = bias_ref[...]
    row_head, _ = _iota_div((rows, ATTN_WIDTH), 0, t_new)
    lane_head, _ = _iota_div((rows, ATTN_WIDTH), 1, HEAD_DIM)
    own = row_head == lane_head
    _, tok = _iota_div((rows, page), 0, t_new)
    new_mask = lax.broadcasted_iota(jnp.int32, (rows, page), 1) < tok
    flat = lambda ref: ref[...].reshape(ATTN_WIDTH, page).astype(BF16)

    def scores(s, mask):
        z = s + bias
        return z, _dot(_softplus(z, mask), from_on), mask

    for j in range(n_seq):
        qx = jnp.where(own, jnp.concatenate([q_ref[j]] * N_HEADS, axis=0), 0.0).astype(BF16)
        for pad, new in ((pad_k, kn_ref), (pad_v, vn_ref)):
            pad[j] = jnp.zeros((page, ATTN_WIDTH), F32)
            pad[j, 0:t_new, :] = new[j]
        pages_k = k_refs[j * n_pages:(j + 1) * n_pages]
        pages_v = v_refs[j * n_pages:(j + 1) * n_pages]
        scored = [scores(_dot_nt(qx, pad_k[j].astype(BF16)), new_mask)]
        scored += [scores(_dot(qx, flat(pages_k[a])), None) for a in range(n_pages)]
        carry = jnp.zeros((rows, 1), F32)
        acc = jnp.zeros((rows, ATTN_WIDTH), F32)
        for i, (z, sums, mask) in enumerate(scored):
            w = _stick_weights(z, sums, carry, mask)
            carry = carry + sums[:, 0:1]
            acc = acc + (_dot(w, pad_v[j].astype(BF16)) if i == 0 else _dot_nt(w, flat(pages_v[i - 1])))
        acc = jnp.where(own, acc, 0.0)
        out = acc[0:t_new]
        for h in range(1, N_HEADS):
            out = out + acc[h * t_new:(h + 1) * t_new]
        o_ref[j] = out


def _attn_sample(q, k_new, v_new, cache_k, cache_v, page_table, bias_rows, layer):
    S, t_new, _ = q.shape
    n_pages = page_table.shape[1]
    page = cache_k.shape[4]
    rows = N_HEADS * t_new
    n_seq = SAMPLE_SEQS_PER_STEP
    assert S % n_seq == 0 and (t_new % BF16_ROWS == 0 or t_new == SUBLANES)
    tok = pl.BlockSpec((n_seq, t_new, ATTN_WIDTH), lambda s, pt: (s, 0, 0))

    def page_spec(j, a):
        return pl.BlockSpec((None, None, N_HEADS, HEAD_DIM, page),
                            lambda s, pt: (layer, pt[s * n_seq + j, n_pages - 1 - a], 0, 0, 0))

    n_blocks = n_seq * n_pages
    page_bytes = N_HEADS * HEAD_DIM * page * 4
    grid_spec = pltpu.PrefetchScalarGridSpec(
        num_scalar_prefetch=1,
        grid=(S // n_seq,),
        in_specs=[pl.BlockSpec((rows, 1), lambda s, pt: (0, 0)), tok, tok, tok]
        + [page_spec(j, a) for j in range(n_seq) for a in range(n_pages)] * 2,
        out_specs=tok,
        scratch_shapes=[pltpu.VMEM((n_seq, page, ATTN_WIDTH), F32), pltpu.VMEM((n_seq, page, ATTN_WIDTH), F32)],
    )
    return pl.pallas_call(
        functools.partial(_attn_sample_kernel, t_new=t_new, page=page, n_pages=n_pages, n_seq=n_seq),
        grid_spec=grid_spec,
        out_shape=jax.ShapeDtypeStruct((S, t_new, ATTN_WIDTH), F32),
        compiler_params=pltpu.CompilerParams(vmem_limit_bytes=4 * n_blocks * page_bytes + 16 * MIB,
                                             dimension_semantics=("arbitrary",)),
        name="attn_sample",
    )(page_table, bias_rows, q, k_new, v_new, *([cache_k] * n_blocks), *([cache_v] * n_blocks))


def _mix_kernel(x_ref, gs_ref, ga_ref, z_ref, o_ref, wbs_ref, wba_ref, wout_ref, g_ref, b_ref, y_ref, *, alpha):
    merged = (gs_ref[...] * _dot(z_ref[...].astype(BF16), wbs_ref[...])
              + ga_ref[...] * _dot(o_ref[...].astype(BF16), wba_ref[...]))
    mix = _dot(merged.astype(BF16), wout_ref[...])
    y_ref[...] = _layer_norm(alpha * x_ref[...] + mix, g_ref[...], b_ref[...])


def _mix(x3, gs, ga, z, o, wbs, wba, wout, ln_g, ln_b, *, tm, alpha):
    G, T, _ = x3.shape
    row = lambda w: pl.BlockSpec((None, tm, w), lambda g, i: (g, i, 0))
    return pl.pallas_call(
        functools.partial(_mix_kernel, alpha=alpha),
        grid=(G, T // tm),
        in_specs=[row(D_MODEL), row(D_MODEL), row(D_MODEL),
                  row(SSM_WIDTH), row(ATTN_WIDTH),
                  _const_spec((SSM_WIDTH, D_MODEL)), _const_spec((ATTN_WIDTH, D_MODEL)),
                  _const_spec((D_MODEL, D_MODEL)), _const_spec((1, D_MODEL)), _const_spec((1, D_MODEL))],
        out_specs=row(D_MODEL),
        out_shape=jax.ShapeDtypeStruct((G, T, D_MODEL), F32),
        compiler_params=_vmem(48 * MIB),
        name="mix",
    )(x3, gs, ga, z, o, wbs, wba, wout, ln_g, ln_b)


def _chunk_major(a):
    *lead, rows, cols = a.shape
    return jnp.swapaxes(a.reshape(*lead, rows, cols // FF_CHUNK, FF_CHUNK), -3, -2)


def _from_chunk_major(a):
    *lead, n, rows, _ = a.shape
    return jnp.swapaxes(a, -3, -2).reshape(*lead, rows, n * FF_CHUNK)


def _ffn_kernel(x_ref, prev_ref, wup_ref, cw_ref, cb_ref, wdown_ref, g_ref, b_ref,
                y_ref, up_ref, acc_ref, xb_ref, ua_ref, ub_ref, *, alpha, seq_len, carry_rows):
    tm = x_ref.shape[0]
    xb_ref[...] = x_ref[...].astype(BF16)
    if seq_len == tm:
        t_loc = lax.broadcasted_iota(jnp.int32, (tm, 1), 0)
    else:
        _, t_loc = _iota_div((tm, 1), 0, seq_len)
    if carry_rows:
        @pl.when(pl.program_id(1) == 0)
        def _():
            up_ref[...] = prev_ref[...]

    def project(c, u_ref):
        u_ref[0] = _dot(xb_ref[...], wup_ref[c])
        u_ref[1] = _dot(xb_ref[...], wup_ref[c + N_FF_CHUNKS])

    def conv(j, up):
        if carry_rows:
            before2 = jnp.broadcast_to(up_ref[j, SUBLANES - 2:SUBLANES - 1, :], (SUBLANES, FF_CHUNK))
            before1 = jnp.broadcast_to(up_ref[j, SUBLANES - 1:SUBLANES, :], (SUBLANES, FF_CHUNK))
            t8 = t_loc[0:SUBLANES]
            up1, up2 = pltpu.roll(up, 1, axis=0), pltpu.roll(up, 2, axis=0)
            head1 = jnp.where(t8 >= 1, up1[0:SUBLANES], before1)
            head2 = jnp.where(t8 >= 2, up2[0:SUBLANES], jnp.where(t8 == 0, before2, before1))
            up1 = jnp.concatenate([head1, up1[SUBLANES:]], axis=0)
            up2 = jnp.concatenate([head2, up2[SUBLANES:]], axis=0)
            up_ref[j] = up[tm - SUBLANES:tm, :]
        else:
            n_seqs = tm // seq_len
            per_row = lambda s: jnp.broadcast_to(s[:, None, :], (n_seqs, seq_len, FF_CHUNK)).reshape(tm, FF_CHUNK)
            before2, before1 = per_row(prev_ref[0, j]), per_row(prev_ref[1, j])
            up1 = jnp.where(t_loc >= 1, pltpu.roll(up, 1, axis=0), before1)
            up2 = jnp.where(t_loc >= 2, pltpu.roll(up, 2, axis=0), jnp.where(t_loc == 0, before2, before1))
            by_seq = up.reshape(n_seqs, seq_len, FF_CHUNK)
            up_ref[0, j] = by_seq[:, seq_len - 2, :]
            up_ref[1, j] = by_seq[:, seq_len - 1, :]
        cw = cw_ref[j]
        return cb_ref[j] + cw[0:1] * up2 + cw[1:2] * up1 + cw[2:3] * up

    def down(c, u_ref):
        h = _gelu(conv(c, u_ref[0])) * conv(c + N_FF_CHUNKS, u_ref[1])
        return _dot(h.astype(BF16), wdown_ref[c])

    assert N_FF_CHUNKS % 2 == 1
    project(0, ua_ref)
    acc_ref[...] = jnp.zeros_like(acc_ref)

    def pair(i, _):
        project(2 * i + 1, ub_ref)
        acc_ref[...] += down(2 * i, ua_ref)
        project(2 * i + 2, ua_ref)
        acc_ref[...] += down(2 * i + 1, ub_ref)
        return 0

    lax.fori_loop(0, N_FF_CHUNKS // 2, pair, 0, unroll=FFN_PAIR_UNROLL)
    last = down(N_FF_CHUNKS - 1, ua_ref)
    y_ref[...] = _layer_norm(alpha * x_ref[...] + (acc_ref[...] + last), g_ref[...], b_ref[...])


def _ffn(x3, prev, wup, conv_w, conv_b, wdown, ln_g, ln_b, *, tm, alpha, seq_len, carry_rows):
    G, T, _ = x3.shape
    n2 = 2 * N_FF_CHUNKS
    row = lambda w: pl.BlockSpec((None, tm, w), lambda g, i: (g, i, 0))
    if carry_rows:
        up_spec = pl.BlockSpec((None, n2, SUBLANES, FF_CHUNK), lambda g, i: (g, 0, 0, 0))
        up_shape = jax.ShapeDtypeStruct((G, n2, SUBLANES, FF_CHUNK), F32)
    else:
        up_spec = pl.BlockSpec((None, CONV_W - 1, n2, tm // seq_len, FF_CHUNK), lambda g, i: (g, 0, 0, i, 0))
        up_shape = jax.ShapeDtypeStruct((G, CONV_W - 1, n2, T // seq_len, FF_CHUNK), F32)
    return pl.pallas_call(
        functools.partial(_ffn_kernel, alpha=alpha, seq_len=seq_len, carry_rows=carry_rows),
        grid=(G, T // tm),
        in_specs=[row(D_MODEL), up_spec, _const_spec((n2, D_MODEL, FF_CHUNK)),
                  _const_spec((n2, CONV_W, FF_CHUNK)), _const_spec((n2, 1, FF_CHUNK)),
                  _const_spec((N_FF_CHUNKS, FF_CHUNK, D_MODEL)), _const_spec((1, D_MODEL)),
                  _const_spec((1, D_MODEL))],
        out_specs=[row(D_MODEL), up_spec],
        out_shape=[jax.ShapeDtypeStruct((G, T, D_MODEL), F32), up_shape],
        scratch_shapes=[pltpu.VMEM((tm, D_MODEL), F32), pltpu.VMEM((tm, D_MODEL), BF16),
                        pltpu.VMEM((2, tm, FF_CHUNK), F32), pltpu.VMEM((2, tm, FF_CHUNK), F32)],
        compiler_params=pltpu.CompilerParams(vmem_limit_bytes=56 * MIB,
                                             dimension_semantics=("arbitrary", "arbitrary")),
        name="conv_ffn",
    )(x3, prev, wup, conv_w, conv_b, wdown, ln_g, ln_b)


def _largest_divisor(n, cap, multiple):
    best = None
    for d in range(multiple, cap + 1, multiple):
        if n % d == 0:
            best = d
    assert best is not None, (n, cap, multiple)
    return best


def _layer_weights(l, w_in, ssm, w_branch_ssm, w_branch_attn, w_out, ln1_g, ln1_b, w_up, conv_w, conv_b,
                   w_down, ln2_g, ln2_b, sb_bias):
    row = lambda a: a[l].reshape(1, -1)
    return {
        "w_in": w_in[l].astype(BF16), "s5": _s5_params(*[a[l] for a in ssm]),
        "wbs": w_branch_ssm[l].astype(BF16), "wba": w_branch_attn[l].astype(BF16),
        "wout": w_out[l].astype(BF16), "ln1_g": row(ln1_g), "ln1_b": row(ln1_b),
        "wup": _chunk_major(w_up[l].astype(BF16)), "conv_w": _chunk_major(conv_w[l]),
        "conv_b": _chunk_major(row(conv_b)),
        "wdown": w_down[l].astype(BF16).reshape(N_FF_CHUNKS, FF_CHUNK, D_MODEL),
        "ln2_g": row(ln2_g), "ln2_b": row(ln2_b), "sb_bias": sb_bias[l],
    }


def kernel(x_prompt, x_sample, cache_k, cache_v, state_ssm_re, state_ssm_im, state_conv, page_table, meta_tokens, ln_in_g, ln_in_b, w_in, ssm_a_re, ssm_a_im, ssm_log_dt, ssm_b_re, ssm_b_im, ssm_c_re, ssm_c_im, ssm_d, w_glu, b_glu, w_branch_ssm, w_branch_attn, w_out, ln1_g, ln1_b, w_up, conv_w, conv_b, w_down, ln2_g, ln2_b, sb_bias):
    B, seq, _ = x_prompt.shape
    S, t_new, _ = x_sample.shape
    depth = w_in.shape[0]
    alpha = (2.0 * depth) ** 0.25
    T = N_META + seq
    assert B % SUBLANES == 0 and S % SUBLANES == 0 and seq % Q_BLOCK == 0 and t_new >= CONV_W - 1
    assert cache_k.shape[3:] == (N_HEADS, HEAD_DIM)
    ck = jnp.transpose(cache_k, (0, 1, 3, 4, 2))
    cv = jnp.transpose(cache_v, (0, 1, 3, 4, 2))

    ssm = (ssm_a_re, ssm_a_im, ssm_log_dt, ssm_b_re, ssm_b_im, ssm_c_re, ssm_c_im, ssm_d, w_glu, b_glu)
    lng, lnb = ln_in_g.reshape(1, -1), ln_in_b.reshape(1, -1)

    tm_p = _largest_divisor(T, PROMPT_ROWS_CAP, BF16_ROWS)
    tc_p = _largest_divisor(T, SCAN_STEPS_CAP, SUBLANES)
    rows_s = S * t_new
    tm_s = _largest_divisor(rows_s, SAMPLE_ROWS_CAP, BF16_ROWS * t_new)

    meta = jnp.broadcast_to(meta_tokens[None], (B, N_META, D_MODEL))
    xp = jnp.concatenate([meta, x_prompt], axis=1)
    xs = x_sample.reshape(1, rows_s, D_MODEL)
    zeros_state = jnp.zeros((B, N_STATE), F32)
    zeros_conv = jnp.zeros((B, 2 * N_FF_CHUNKS, SUBLANES, FF_CHUNK), F32)

    outs_p, outs_s = [], []
    for l in range(depth):
        w = _layer_weights(l, w_in, ssm, w_branch_ssm, w_branch_attn, w_out, ln1_g, ln1_b, w_up, conv_w,
                           conv_b, w_down, ln2_g, ln2_b, sb_bias)
        first = l == 0

        res = _in_proj(xp, lng, lnb, w["w_in"], tm=tm_p, ln_input=first)
        if first:
            xp, *res = res
        u, q, k, v, gs, ga = res
        z, hr, hi = _s5(u, zeros_state, zeros_state, w["s5"], tc=tc_p)
        o = _attn_prompt(q, k, v, w["sb_bias"])
        x1 = _mix(xp, gs, ga, z, o, w["wbs"], w["wba"], w["wout"],
                  w["ln1_g"], w["ln1_b"], tm=tm_p, alpha=alpha)
        xp, tail = _ffn(x1, zeros_conv, w["wup"], w["conv_w"], w["conv_b"], w["wdown"], w["ln2_g"],
                        w["ln2_b"], tm=tm_p, alpha=alpha, seq_len=tm_p, carry_rows=True)
        outs_p.append((k.reshape(B, T, N_HEADS, HEAD_DIM), v.reshape(B, T, N_HEADS, HEAD_DIM),
                       hr.reshape(B, SSM_GROUPS, SSM_STATE), hi.reshape(B, SSM_GROUPS, SSM_STATE),
                       _from_chunk_major(tail)[:, SUBLANES - (CONV_W - 1):]))

        res = _in_proj(xs, lng, lnb, w["w_in"], tm=tm_s, ln_input=first)
        if first:
            xs, *res = res
        u, q, k, v, gs, ga = res
        z, hr, hi = _s5(u.reshape(S, t_new, SSM_WIDTH), state_ssm_re[l].reshape(S, N_STATE),
                        state_ssm_im[l].reshape(S, N_STATE), w["s5"], tc=t_new)
        k3, v3 = k.reshape(S, t_new, ATTN_WIDTH), v.reshape(S, t_new, ATTN_WIDTH)
        bias_rows = jnp.repeat(w["sb_bias"], t_new).reshape(N_HEADS * t_new, 1)
        o = _attn_sample(q.reshape(S, t_new, ATTN_WIDTH), k3, v3, ck, cv, page_table, bias_rows, l)
        x1 = _mix(xs, gs, ga, z.reshape(1, rows_s, SSM_WIDTH), o.reshape(1, rows_s, ATTN_WIDTH), w["wbs"],
                  w["wba"], w["wout"],
                  w["ln1_g"], w["ln1_b"], tm=tm_s, alpha=alpha)
        prev = state_conv[l].reshape(S, CONV_W - 1, 2 * N_FF_CHUNKS, FF_CHUNK).transpose(1, 2, 0, 3)[None]
        xs, up = _ffn(x1, prev, w["wup"], w["conv_w"], w["conv_b"], w["wdown"], w["ln2_g"],
                      w["ln2_b"], tm=tm_s, alpha=alpha, seq_len=t_new, carry_rows=False)
        outs_s.append((k3.reshape(S, t_new, N_HEADS, HEAD_DIM), v3.reshape(S, t_new, N_HEADS, HEAD_DIM),
                       hr.reshape(S, SSM_GROUPS, SSM_STATE), hi.reshape(S, SSM_GROUPS, SSM_STATE),
                       up[0].transpose(2, 0, 1, 3).reshape(S, CONV_W - 1, 2 * D_FF)))

    stack = lambda outs, j: jnp.stack([o[j] for o in outs])
    return (xp[:, N_META:], xs.reshape(S, t_new, D_MODEL),
            *[stack(outs_p, j) for j in range(5)], *[stack(outs_s, j) for j in range(5)])
```

```python
import functools
import math

import jax
import jax.numpy as jnp
from jax import lax
from jax.experimental import pallas as pl
from jax.experimental.pallas import tpu as pltpu

F32 = jnp.float32
BF16 = jnp.bfloat16

D_MODEL = 1024
N_META = 16
SSM_WIDTH = 512
SSM_GROUP = 16
SSM_GROUPS = 32
SSM_STATE = 64
N_STATE = SSM_GROUPS * SSM_STATE
HEAD_DIM = 64
N_HEADS = 8
ATTN_WIDTH = 512
D_FF = 2816
CONV_W = 3
LN_EPS = 1e-5
O_Q = SSM_WIDTH
O_K = O_Q + ATTN_WIDTH
O_V = O_K + ATTN_WIDTH
O_GS = O_V + ATTN_WIDTH
O_GA = O_GS + D_MODEL
IN_COLS = O_GA + D_MODEL

SUBLANES = 8
LANES = 128
BF16_ROWS = 16
MXU_COLS = 256
Q_BLOCK = 256
K_BLOCK = 256
FF_CHUNK = MXU_COLS
N_FF_CHUNKS = D_FF // FF_CHUNK
MIB = 1024 * 1024
PROMPT_ROWS_CAP = 700
SAMPLE_ROWS_CAP = 256
SCAN_STEPS_CAP = 96
FFN_PAIR_UNROLL = N_FF_CHUNKS // 2
SAMPLE_SEQS_PER_STEP = 1


def _vmem(nbytes):
    return pltpu.CompilerParams(vmem_limit_bytes=int(nbytes))


def _const_spec(shape):
    nd = len(shape)
    return pl.BlockSpec(shape, lambda *_: (0,) * nd, pipeline_mode=pl.Buffered(1))


def _layer_norm(x, g, b):
    mu = jnp.mean(x, axis=-1, keepdims=True)
    xc = x - mu
    var = jnp.mean(xc * xc, axis=-1, keepdims=True)
    return xc * lax.rsqrt(var + LN_EPS) * g + b


def _gelu(x):
    return 0.5 * x * (1.0 + lax.erf(x * math.sqrt(0.5)))


def _iota_div(shape, axis, n):
    assert n & (n - 1) == 0, n
    i = lax.broadcasted_iota(jnp.int32, shape, axis)
    return lax.shift_right_logical(i, n.bit_length() - 1), lax.bitwise_and(i, n - 1)


def _dot(a, b):
    return jnp.dot(a, b, preferred_element_type=F32)


def _dot_nt(a, b):
    return lax.dot_general(a, b, (((1,), (1,)), ((), ())), preferred_element_type=F32)


def _in_proj_kernel(x_ref, g_ref, b_ref, w_ref, *out_refs, ln_input):
    if ln_input:
        xn_ref, u_ref, q_ref, k_ref, v_ref, gs_ref, ga_ref = out_refs
    else:
        u_ref, q_ref, k_ref, v_ref, gs_ref, ga_ref = out_refs
    x = x_ref[...]
    if ln_input:
        x = _layer_norm(x, g_ref[...], b_ref[...])
        xn_ref[...] = x
    xb = x.astype(BF16)
    u_ref[...] = _dot(xb, w_ref[:, 0:O_Q])
    q_ref[...] = _dot(xb, w_ref[:, O_Q:O_K]) * (HEAD_DIM ** -0.5)
    k_ref[...] = _dot(xb, w_ref[:, O_K:O_V])
    v_ref[...] = _dot(xb, w_ref[:, O_V:O_GS])
    gs_ref[...] = jax.nn.sigmoid(_dot(xb, w_ref[:, O_GS:O_GA])).astype(gs_ref.dtype)
    ga_ref[...] = jax.nn.sigmoid(_dot(xb, w_ref[:, O_GA:IN_COLS])).astype(ga_ref.dtype)


def _in_proj(x3, ln_g, ln_b, w_in_bf, *, tm, ln_input):
    G, T, _ = x3.shape
    row = lambda w: pl.BlockSpec((None, tm, w), lambda g, i: (g, i, 0))
    out_shape, out_specs = [], []
    if ln_input:
        out_shape.append(jax.ShapeDtypeStruct((G, T, D_MODEL), F32))
        out_specs.append(row(D_MODEL))
    for _ in range(4):
        out_shape.append(jax.ShapeDtypeStruct((G, T, ATTN_WIDTH), F32))
        out_specs.append(row(ATTN_WIDTH))
    for _ in range(2):
        out_shape.append(jax.ShapeDtypeStruct((G, T, D_MODEL), BF16))
        out_specs.append(row(D_MODEL))
    return pl.pallas_call(
        functools.partial(_in_proj_kernel, ln_input=ln_input),
        grid=(G, T // tm),
        in_specs=[row(D_MODEL), _const_spec((1, D_MODEL)), _const_spec((1, D_MODEL)),
                  _const_spec((D_MODEL, IN_COLS))],
        out_specs=out_specs,
        out_shape=out_shape,
        compiler_params=_vmem(56 * MIB),
        name="in_proj",
    )(x3, ln_g, ln_b, w_in_bf)


SCAN_COLS = 512


def _s5_kernel(u_ref, h0r_ref, h0i_ref, ar_ref, ai_ref, wb_ref, wc_ref, d_ref, wglu_ref, bglu_ref,
               z_ref, hr_out, hi_out, xr_ref, xi_ref, hr_s, hi_s, *, nb, tc):
    @pl.when(pl.program_id(0) == 0)
    def _():
        hr_s[...] = h0r_ref[...]
        hi_s[...] = h0i_ref[...]

    half_in = SSM_WIDTH // 2
    half_st = N_STATE // 2
    u = jnp.swapaxes(u_ref[...], 0, 1).reshape(tc * nb, SSM_WIDTH)
    ub = u.astype(BF16)
    for h in range(2):
        ubh = ub[:, h * half_in:(h + 1) * half_in]
        xr_ref[:, h * half_st:(h + 1) * half_st] = _dot(ubh, wb_ref[h, :, :half_st])
        xi_ref[:, h * half_st:(h + 1) * half_st] = _dot(ubh, wb_ref[h, :, half_st:])

    for c in range(N_STATE // SCAN_COLS):
        cols = slice(c * SCAN_COLS, (c + 1) * SCAN_COLS)
        ar = jnp.broadcast_to(ar_ref[:, cols], (SUBLANES, SCAN_COLS))
        ai = jnp.broadcast_to(ai_ref[:, cols], (SUBLANES, SCAN_COLS))

        def seq_group(gi, _):
            r0 = pl.multiple_of(gi * SUBLANES, SUBLANES)

            def step(t, carry):
                hr, hi = carry
                r = pl.multiple_of(t * nb + r0, SUBLANES)
                nhr = ar * hr - ai * hi + xr_ref[pl.ds(r, SUBLANES), cols]
                nhi = ar * hi + ai * hr + xi_ref[pl.ds(r, SUBLANES), cols]
                xr_ref[pl.ds(r, SUBLANES), cols] = nhr
                xi_ref[pl.ds(r, SUBLANES), cols] = nhi
                return nhr, nhi

            hr, hi = lax.fori_loop(0, tc, step,
                                   (hr_s[pl.ds(r0, SUBLANES), cols], hi_s[pl.ds(r0, SUBLANES), cols]),
                                   unroll=2)
            hr_s[pl.ds(r0, SUBLANES), cols] = hr
            hi_s[pl.ds(r0, SUBLANES), cols] = hi
            return 0

        lax.fori_loop(0, nb // SUBLANES, seq_group, 0)

    ys = []
    for h in range(2):
        st = slice(h * half_st, (h + 1) * half_st)
        y = (_dot(xr_ref[:, st].astype(BF16), wc_ref[h, :half_st, :])
             + _dot(xi_ref[:, st].astype(BF16), wc_ref[h, half_st:, :]))
        ch = slice(h * half_in, (h + 1) * half_in)
        ys.append(y + u[:, ch] * d_ref[:, ch])
    y = jnp.concatenate(ys, axis=1)
    g = _gelu(y)
    z = g * jax.nn.sigmoid(_dot(g.astype(BF16), wglu_ref[...]) + bglu_ref[...])
    z_ref[...] = jnp.swapaxes(z.reshape(tc, nb, SSM_WIDTH), 0, 1).astype(z_ref.dtype)
    hr_out[...] = hr_s[...]
    hi_out[...] = hi_s[...]


def _s5(u, h0_re, h0_im, prm, *, tc):
    nb, T, _ = u.shape
    rows = tc * nb
    blk = pl.BlockSpec((nb, tc, SSM_WIDTH), lambda i: (0, i, 0))
    st = _const_spec((nb, N_STATE))
    return pl.pallas_call(
        functools.partial(_s5_kernel, nb=nb, tc=tc),
        grid=(T // tc,),
        in_specs=[blk, st, st, _const_spec((1, N_STATE)), _const_spec((1, N_STATE)),
                  _const_spec((2, SSM_WIDTH // 2, N_STATE)), _const_spec((2, N_STATE, SSM_WIDTH // 2)),
                  _const_spec((1, SSM_WIDTH)), _const_spec((SSM_WIDTH, SSM_WIDTH)),
                  _const_spec((1, SSM_WIDTH))],
        out_specs=[blk, pl.BlockSpec((nb, N_STATE), lambda i: (0, 0)),
                   pl.BlockSpec((nb, N_STATE), lambda i: (0, 0))],
        out_shape=[jax.ShapeDtypeStruct((nb, T, SSM_WIDTH), BF16 if tc % BF16_ROWS == 0 else F32),
                   jax.ShapeDtypeStruct((nb, N_STATE), F32),
                   jax.ShapeDtypeStruct((nb, N_STATE), F32)],
        scratch_shapes=[pltpu.VMEM((rows, N_STATE), F32), pltpu.VMEM((rows, N_STATE), F32),
                        pltpu.VMEM((nb, N_STATE), F32), pltpu.VMEM((nb, N_STATE), F32)],
        compiler_params=pltpu.CompilerParams(vmem_limit_bytes=56 * MIB,
                                             dimension_semantics=("arbitrary",)),
        name="s5_scan",
    )(u, h0_re, h0_im, prm["abar_re"], prm["abar_im"], prm["wb"], prm["wc"], prm["d"],
      prm["w_glu"], prm["b_glu"])


def _s5_params(a_re, a_im, log_dt, b_re, b_im, c_re, c_im, d_skip, w_glu, b_glu):
    dt = jnp.exp(log_dt)[:, None]
    mag = jnp.exp(a_re * dt)
    abar_re, abar_im = mag * jnp.cos(a_im * dt), mag * jnp.sin(a_im * dt)
    nr, ni = abar_re - 1.0, abar_im
    den = a_re * a_re + a_im * a_im
    s_re = (nr * a_re + ni * a_im) / den
    s_im = (ni * a_re - nr * a_im) / den
    bb_re = s_re[..., None] * b_re - s_im[..., None] * b_im
    bb_im = s_re[..., None] * b_im + s_im[..., None] * b_re
    eye = jnp.eye(SSM_GROUPS, dtype=F32)
    wb_re = jnp.einsum('gpk,gh->gkhp', bb_re, eye).reshape(SSM_WIDTH, N_STATE)
    wb_im = jnp.einsum('gpk,gh->gkhp', bb_im, eye).reshape(SSM_WIDTH, N_STATE)
    wc_re = jnp.einsum('gkp,gh->gphk', c_re, eye).reshape(N_STATE, SSM_WIDTH)
    wc_im = jnp.einsum('gkp,gh->gphk', c_im, eye).reshape(N_STATE, SSM_WIDTH)
    hi_, hs = SSM_WIDTH // 2, N_STATE // 2
    wb = jnp.stack([jnp.concatenate([wb_re[h * hi_:(h + 1) * hi_, h * hs:(h + 1) * hs],
                                     wb_im[h * hi_:(h + 1) * hi_, h * hs:(h + 1) * hs]], axis=1)
                    for h in range(2)])
    wc = jnp.stack([jnp.concatenate([wc_re[h * hs:(h + 1) * hs, h * hi_:(h + 1) * hi_],
                                     -wc_im[h * hs:(h + 1) * hs, h * hi_:(h + 1) * hi_]], axis=0)
                    for h in range(2)])
    return {"abar_re": abar_re.reshape(1, N_STATE), "abar_im": abar_im.reshape(1, N_STATE),
            "wb": wb.astype(BF16), "wc": wc.astype(BF16), "d": d_skip.reshape(1, SSM_WIDTH),
            "w_glu": w_glu.astype(BF16), "b_glu": b_glu.reshape(1, SSM_WIDTH)}


def _from_key_on_matrix(n):
    m = lax.broadcasted_iota(jnp.int32, (n, n), 0)
    j = lax.broadcasted_iota(jnp.int32, (n, n), 1)
    return jnp.where(m > j, 1.0, 0.0).astype(BF16)


def _softplus(z, mask):
    p = jnp.maximum(z, 0.0) + jnp.log(1.0 + jnp.exp(-jnp.abs(z)))
    log_beta = z - p
    if mask is not None:
        p = jnp.where(mask, p, 0.0)
    return log_beta, p.astype(BF16), p[:, 0:1]


def _stick_weights(z, sums, carry, mask):
    w = jnp.exp(z - (sums + carry))
    if mask is not None:
        w = jnp.where(mask, w, 0.0)
    return w.astype(BF16)


def _attn_prompt_kernel(bias_ref, q_ref, k_ref, v_ref, o_ref, kb_ref, vb_ref, from_ref, qm_ref, z_ref, p_ref,
                        w_ref, car_ref, p0_ref, acc_ref, *, seq_len):
    t_pad = kb_ref.shape[0]
    for src, dst in ((k_ref, kb_ref), (v_ref, vb_ref)):
        dst[0:seq_len, :] = src[...].astype(BF16)
        dst[seq_len:t_pad, :] = jnp.zeros((t_pad - seq_len, ATTN_WIDTH), BF16)
    from_ref[...] = _from_key_on_matrix(K_BLOCK)
    lane_head, _ = _iota_div((1, LANES), 1, HEAD_DIM)
    n_chunks = ATTN_WIDTH // LANES
    chunk = lambda c: slice(c * LANES, (c + 1) * LANES)
    bias = [bias_ref[h] for h in range(N_HEADS)]

    def q_block(r0, tq, k_diag, n_full):
        qi = lax.broadcasted_iota(jnp.int32, (tq, K_BLOCK), 0) + (r0 - k_diag)
        kj = lax.broadcasted_iota(jnp.int32, (tq, K_BLOCK), 1)
        diag_mask = kj < qi
        rows = lambda h: slice(h * Q_BLOCK, h * Q_BLOCK + tq)
        for c in range(n_chunks):
            qc = q_ref[pl.ds(r0, tq), chunk(c)]
            for e in range(2):
                qm_ref[2 * c + e, 0:tq, :] = jnp.where(lane_head == e, qc, 0.0).astype(BF16)

        def sweep(k0, mask, first):
            for h in range(N_HEADS):
                z = _dot_nt(qm_ref[h, 0:tq, :], kb_ref[pl.ds(k0, K_BLOCK), chunk(h // 2)]) + bias[h]
                z_ref[h, 0:tq, :], p_ref[rows(h), :], p0_ref[h, 0:tq, :] = _softplus(z, mask)
            if tq == Q_BLOCK:
                sums = _dot(p_ref[...], from_ref[...])
            for h in range(N_HEADS):
                sums_h = sums[rows(h)] if tq == Q_BLOCK else _dot(p_ref[rows(h), :], from_ref[...])
                carry = jnp.zeros((tq, 1), F32) if first else car_ref[h, 0:tq, :]
                w_ref[h, 0:tq, :] = _stick_weights(z_ref[h, 0:tq, :], sums_h, carry, mask)
                car_ref[h, 0:tq, :] = carry + sums_h[:, 0:1] + p0_ref[h, 0:tq, :]
            for h in range(N_HEADS):
                pv = _dot(w_ref[h, 0:tq, :], vb_ref[pl.ds(k0, K_BLOCK), chunk(h // 2)])
                acc_ref[h, 0:tq, :] = pv if first else acc_ref[h, 0:tq, :] + pv

        sweep(k_diag, diag_mask, True)

        def full(jj, _):
            sweep(pl.multiple_of(k_diag - (jj + 1) * K_BLOCK, K_BLOCK), None, False)
            return 0

        lax.fori_loop(0, n_full, full, 0)
        for c in range(n_chunks):
            o_ref[pl.ds(r0, tq), chunk(c)] = jnp.where(
                lane_head == 0, acc_ref[2 * c, 0:tq, :], acc_ref[2 * c + 1, 0:tq, :]).astype(o_ref.dtype)

    ratio = K_BLOCK // Q_BLOCK
    n_q = seq_len // Q_BLOCK

    def whole(i, _):
        jd = lax.shift_right_logical(i, ratio.bit_length() - 1)
        q_block(pl.multiple_of(i * Q_BLOCK, Q_BLOCK), Q_BLOCK, pl.multiple_of(jd * K_BLOCK, K_BLOCK), jd)
        return 0

    lax.fori_loop(0, n_q, whole, 0)
    if seq_len > n_q * Q_BLOCK:
        r0 = n_q * Q_BLOCK
        q_block(r0, seq_len - r0, r0 // K_BLOCK * K_BLOCK, r0 // K_BLOCK)


def _attn_prompt(q, k, v, bias):
    B, T, _ = q.shape
    assert K_BLOCK % Q_BLOCK == 0 and (K_BLOCK // Q_BLOCK) & (K_BLOCK // Q_BLOCK - 1) == 0
    assert (T % Q_BLOCK) % BF16_ROWS == 0
    t_pad = -(-T // K_BLOCK) * K_BLOCK
    blk = pl.BlockSpec((None, T, ATTN_WIDTH), lambda b: (b, 0, 0))
    return pl.pallas_call(
        functools.partial(_attn_prompt_kernel, seq_len=T),
        grid=(B,),
        in_specs=[pl.BlockSpec(memory_space=pltpu.SMEM), blk, blk, blk],
        out_specs=blk,
        out_shape=jax.ShapeDtypeStruct((B, T, ATTN_WIDTH), BF16),
        scratch_shapes=[pltpu.VMEM((t_pad, ATTN_WIDTH), BF16), pltpu.VMEM((t_pad, ATTN_WIDTH), BF16),
                        pltpu.VMEM((K_BLOCK, K_BLOCK), BF16),
                        pltpu.VMEM((N_HEADS, Q_BLOCK, LANES), BF16),
                        pltpu.VMEM((N_HEADS, Q_BLOCK, K_BLOCK), F32),
                        pltpu.VMEM((N_HEADS * Q_BLOCK, K_BLOCK), BF16),
                        pltpu.VMEM((N_HEADS, Q_BLOCK, K_BLOCK), BF16),
                        pltpu.VMEM((N_HEADS, Q_BLOCK, 1), F32),
                        pltpu.VMEM((N_HEADS, Q_BLOCK, 1), F32),
                        pltpu.VMEM((N_HEADS, Q_BLOCK, LANES), F32)],
        compiler_params=_vmem(56 * MIB),
        name="attn_prompt",
    )(bias, q, k, v)


def _attn_sample_kernel(pt_ref, bias_ref, q_ref, kn_ref, vn_ref, *rest, t_new, page, n_pages, n_seq):
    k_refs, v_refs = rest[:n_seq * n_pages], rest[n_seq * n_pages:2 * n_seq * n_pages]
    o_ref, pad_k, pad_v = rest[2 * n_seq * n_pages:]
    rows = N_HEADS * t_new
    from_on = _from_key_on_matrix(page)
    bias = bias_ref[...]
    row_head, _ = _iota_div((rows, ATTN_WIDTH), 0, t_new)
    lane_head, _ = _iota_div((rows, ATTN_WIDTH), 1, HEAD_DIM)
    own = row_head == lane_head
    _, tok = _iota_div((rows, page), 0, t_new)
    new_mask = lax.broadcasted_iota(jnp.int32, (rows, page), 1) < tok
    flat = lambda ref: ref[...].reshape(ATTN_WIDTH, page).astype(BF16)

    def scores(s, mask):
        log_beta, p, p0 = _softplus(s + bias, mask)
        return log_beta, _dot(p, from_on), p0, mask

    for j in range(n_seq):
        qx = jnp.where(own, jnp.concatenate([q_ref[j]] * N_HEADS, axis=0), 0.0).astype(BF16)
        for pad, new in ((pad_k, kn_ref), (pad_v, vn_ref)):
            pad[j] = jnp.zeros((page, ATTN_WIDTH), F32)
            pad[j, 0:t_new, :] = new[j]
        pages_k = k_refs[j * n_pages:(j + 1) * n_pages]
        pages_v = v_refs[j * n_pages:(j + 1) * n_pages]
        scored = [scores(_dot_nt(qx, pad_k[j].astype(BF16)), new_mask)]
        scored += [scores(_dot(qx, flat(pages_k[a])), None) for a in range(n_pages)]
        carry = jnp.zeros((rows, 1), F32)
        acc = jnp.zeros((rows, ATTN_WIDTH), F32)
        for i, (log_beta, sums, p0, mask) in enumerate(scored):
            w = _stick_weights(log_beta, sums, carry, mask)
            carry = carry + sums[:, 0:1] + p0
            acc = acc + (_dot(w, pad_v[j].astype(BF16)) if i == 0 else _dot_nt(w, flat(pages_v[i - 1])))
        acc = jnp.where(own, acc, 0.0)
        out = acc[0:t_new]
        for h in range(1, N_HEADS):
            out = out + acc[h * t_new:(h + 1) * t_new]
        o_ref[j] = out


def _attn_sample(q, k_new, v_new, cache_k, cache_v, page_table, bias_rows, layer):
    S, t_new, _ = q.shape
    n_pages = page_table.shape[1]
    page = cache_k.shape[4]
    rows = N_HEADS * t_new
    n_seq = SAMPLE_SEQS_PER_STEP
    assert S % n_seq == 0 and (t_new % BF16_ROWS == 0 or t_new == SUBLANES)
    tok = pl.BlockSpec((n_seq, t_new, ATTN_WIDTH), lambda s, pt: (s, 0, 0))

    def page_spec(j, a):
        return pl.BlockSpec((None, None, N_HEADS, HEAD_DIM, page),
                            lambda s, pt: (layer, pt[s * n_seq + j, n_pages - 1 - a], 0, 0, 0))

    n_blocks = n_seq * n_pages
    page_bytes = N_HEADS * HEAD_DIM * page * 4
    grid_spec = pltpu.PrefetchScalarGridSpec(
        num_scalar_prefetch=1,
        grid=(S // n_seq,),
        in_specs=[pl.BlockSpec((rows, 1), lambda s, pt: (0, 0)), tok, tok, tok]
        + [page_spec(j, a) for j in range(n_seq) for a in range(n_pages)] * 2,
        out_specs=tok,
        scratch_shapes=[pltpu.VMEM((n_seq, page, ATTN_WIDTH), F32), pltpu.VMEM((n_seq, page, ATTN_WIDTH), F32)],
    )
    return pl.pallas_call(
        functools.partial(_attn_sample_kernel, t_new=t_new, page=page, n_pages=n_pages, n_seq=n_seq),
        grid_spec=grid_spec,
        out_shape=jax.ShapeDtypeStruct((S, t_new, ATTN_WIDTH), F32),
        compiler_params=pltpu.CompilerParams(vmem_limit_bytes=4 * n_blocks * page_bytes + 16 * MIB,
                                             dimension_semantics=("arbitrary",)),
        name="attn_sample",
    )(page_table, bias_rows, q, k_new, v_new, *([cache_k] * n_blocks), *([cache_v] * n_blocks))


def _mix_kernel(x_ref, gs_ref, ga_ref, z_ref, o_ref, wbs_ref, wba_ref, wout_ref, g_ref, b_ref, y_ref, *, alpha):
    merged = (gs_ref[...] * _dot(z_ref[...].astype(BF16), wbs_ref[...])
              + ga_ref[...] * _dot(o_ref[...].astype(BF16), wba_ref[...]))
    mix = _dot(merged.astype(BF16), wout_ref[...])
    y_ref[...] = _layer_norm(alpha * x_ref[...] + mix, g_ref[...], b_ref[...])


def _mix(x3, gs, ga, z, o, wbs, wba, wout, ln_g, ln_b, *, tm, alpha):
    G, T, _ = x3.shape
    row = lambda w: pl.BlockSpec((None, tm, w), lambda g, i: (g, i, 0))
    return pl.pallas_call(
        functools.partial(_mix_kernel, alpha=alpha),
        grid=(G, T // tm),
        in_specs=[row(D_MODEL), row(D_MODEL), row(D_MODEL),
                  row(SSM_WIDTH), row(ATTN_WIDTH),
                  _const_spec((SSM_WIDTH, D_MODEL)), _const_spec((ATTN_WIDTH, D_MODEL)),
                  _const_spec((D_MODEL, D_MODEL)), _const_spec((1, D_MODEL)), _const_spec((1, D_MODEL))],
        out_specs=row(D_MODEL),
        out_shape=jax.ShapeDtypeStruct((G, T, D_MODEL), F32),
        compiler_params=_vmem(48 * MIB),
        name="mix",
    )(x3, gs, ga, z, o, wbs, wba, wout, ln_g, ln_b)


def _chunk_major(a):
    *lead, rows, cols = a.shape
    return jnp.swapaxes(a.reshape(*lead, rows, cols // FF_CHUNK, FF_CHUNK), -3, -2)


def _from_chunk_major(a):
    *lead, n, rows, _ = a.shape
    return jnp.swapaxes(a, -3, -2).reshape(*lead, rows, n * FF_CHUNK)


def _ffn_kernel(x_ref, prev_ref, wup_ref, cw_ref, cb_ref, wdown_ref, g_ref, b_ref,
                y_ref, up_ref, acc_ref, xb_ref, ua_ref, ub_ref, *, alpha, seq_len, carry_rows):
    tm = x_ref.shape[0]
    xb_ref[...] = x_ref[...].astype(BF16)
    if seq_len == tm:
        t_loc = lax.broadcasted_iota(jnp.int32, (tm, 1), 0)
    else:
        _, t_loc = _iota_div((tm, 1), 0, seq_len)
    if carry_rows:
        @pl.when(pl.program_id(1) == 0)
        def _():
            up_ref[...] = prev_ref[...]

    def project(c, u_ref):
        u_ref[0] = _dot(xb_ref[...], wup_ref[c])
        u_ref[1] = _dot(xb_ref[...], wup_ref[c + N_FF_CHUNKS])

    def conv(j, up):
        if carry_rows:
            before2 = jnp.broadcast_to(up_ref[j, SUBLANES - 2:SUBLANES - 1, :], (SUBLANES, FF_CHUNK))
            before1 = jnp.broadcast_to(up_ref[j, SUBLANES - 1:SUBLANES, :], (SUBLANES, FF_CHUNK))
            t8 = t_loc[0:SUBLANES]
            up1, up2 = pltpu.roll(up, 1, axis=0), pltpu.roll(up, 2, axis=0)
            head1 = jnp.where(t8 >= 1, up1[0:SUBLANES], before1)
            head2 = jnp.where(t8 >= 2, up2[0:SUBLANES], jnp.where(t8 == 0, before2, before1))
            up1 = jnp.concatenate([head1, up1[SUBLANES:]], axis=0)
            up2 = jnp.concatenate([head2, up2[SUBLANES:]], axis=0)
            up_ref[j] = up[tm - SUBLANES:tm, :]
        else:
            n_seqs = tm // seq_len
            per_row = lambda s: jnp.broadcast_to(s[:, None, :], (n_seqs, seq_len, FF_CHUNK)).reshape(tm, FF_CHUNK)
            before2, before1 = per_row(prev_ref[0, j]), per_row(prev_ref[1, j])
            up1 = jnp.where(t_loc >= 1, pltpu.roll(up, 1, axis=0), before1)
            up2 = jnp.where(t_loc >= 2, pltpu.roll(up, 2, axis=0), jnp.where(t_loc == 0, before2, before1))
            by_seq = up.reshape(n_seqs, seq_len, FF_CHUNK)
            up_ref[0, j] = by_seq[:, seq_len - 2, :]
            up_ref[1, j] = by_seq[:, seq_len - 1, :]
        cw = cw_ref[j]
        return cb_ref[j] + cw[0:1] * up2 + cw[1:2] * up1 + cw[2:3] * up

    def down(c, u_ref):
        h = _gelu(conv(c, u_ref[0])) * conv(c + N_FF_CHUNKS, u_ref[1])
        return _dot(h.astype(BF16), wdown_ref[c])

    assert N_FF_CHUNKS % 2 == 1
    project(0, ua_ref)
    acc_ref[...] = jnp.zeros_like(acc_ref)

    def pair(i, _):
        project(2 * i + 1, ub_ref)
        acc_ref[...] += down(2 * i, ua_ref)
        project(2 * i + 2, ua_ref)
        acc_ref[...] += down(2 * i + 1, ub_ref)
        return 0

    lax.fori_loop(0, N_FF_CHUNKS // 2, pair, 0, unroll=FFN_PAIR_UNROLL)
    last = down(N_FF_CHUNKS - 1, ua_ref)
    y_ref[...] = _layer_norm(alpha * x_ref[...] + (acc_ref[...] + last), g_ref[...], b_ref[...])


def _ffn(x3, prev, wup, conv_w, conv_b, wdown, ln_g, ln_b, *, tm, alpha, seq_len, carry_rows):
    G, T, _ = x3.shape
    n2 = 2 * N_FF_CHUNKS
    row = lambda w: pl.BlockSpec((None, tm, w), lambda g, i: (g, i, 0))
    if carry_rows:
        up_spec = pl.BlockSpec((None, n2, SUBLANES, FF_CHUNK), lambda g, i: (g, 0, 0, 0))
        up_shape = jax.ShapeDtypeStruct((G, n2, SUBLANES, FF_CHUNK), F32)
    else:
        up_spec = pl.BlockSpec((None, CONV_W - 1, n2, tm // seq_len, FF_CHUNK), lambda g, i: (g, 0, 0, i, 0))
        up_shape = jax.ShapeDtypeStruct((G, CONV_W - 1, n2, T // seq_len, FF_CHUNK), F32)
    return pl.pallas_call(
        functools.partial(_ffn_kernel, alpha=alpha, seq_len=seq_len, carry_rows=carry_rows),
        grid=(G, T // tm),
        in_specs=[row(D_MODEL), up_spec, _const_spec((n2, D_MODEL, FF_CHUNK)),
                  _const_spec((n2, CONV_W, FF_CHUNK)), _const_spec((n2, 1, FF_CHUNK)),
                  _const_spec((N_FF_CHUNKS, FF_CHUNK, D_MODEL)), _const_spec((1, D_MODEL)),
                  _const_spec((1, D_MODEL))],
        out_specs=[row(D_MODEL), up_spec],
        out_shape=[jax.ShapeDtypeStruct((G, T, D_MODEL), F32), up_shape],
        scratch_shapes=[pltpu.VMEM((tm, D_MODEL), F32), pltpu.VMEM((tm, D_MODEL), BF16),
                        pltpu.VMEM((2, tm, FF_CHUNK), F32), pltpu.VMEM((2, tm, FF_CHUNK), F32)],
        compiler_params=pltpu.CompilerParams(vmem_limit_bytes=56 * MIB,
                                             dimension_semantics=("arbitrary", "arbitrary")),
        name="conv_ffn",
    )(x3, prev, wup, conv_w, conv_b, wdown, ln_g, ln_b)


def _largest_divisor(n, cap, multiple):
    best = None
    for d in range(multiple, cap + 1, multiple):
        if n % d == 0:
            best = d
    assert best is not None, (n, cap, multiple)
    return best


def _layer_weights(l, w_in, ssm, w_branch_ssm, w_branch_attn, w_out, ln1_g, ln1_b, w_up, conv_w, conv_b,
                   w_down, ln2_g, ln2_b, sb_bias):
    row = lambda a: a[l].reshape(1, -1)
    return {
        "w_in": w_in[l].astype(BF16), "s5": _s5_params(*[a[l] for a in ssm]),
        "wbs": w_branch_ssm[l].astype(BF16), "wba": w_branch_attn[l].astype(BF16),
        "wout": w_out[l].astype(BF16), "ln1_g": row(ln1_g), "ln1_b": row(ln1_b),
        "wup": _chunk_major(w_up[l].astype(BF16)), "conv_w": _chunk_major(conv_w[l]),
        "conv_b": _chunk_major(row(conv_b)),
        "wdown": w_down[l].astype(BF16).reshape(N_FF_CHUNKS, FF_CHUNK, D_MODEL),
        "ln2_g": row(ln2_g), "ln2_b": row(ln2_b), "sb_bias": sb_bias[l],
    }


def kernel(x_prompt, x_sample, cache_k, cache_v, state_ssm_re, state_ssm_im, state_conv, page_table, meta_tokens, ln_in_g, ln_in_b, w_in, ssm_a_re, ssm_a_im, ssm_log_dt, ssm_b_re, ssm_b_im, ssm_c_re, ssm_c_im, ssm_d, w_glu, b_glu, w_branch_ssm, w_branch_attn, w_out, ln1_g, ln1_b, w_up, conv_w, conv_b, w_down, ln2_g, ln2_b, sb_bias):
    B, seq, _ = x_prompt.shape
    S, t_new, _ = x_sample.shape
    depth = w_in.shape[0]
    alpha = (2.0 * depth) ** 0.25
    T = N_META + seq
    assert B % SUBLANES == 0 and S % SUBLANES == 0 and seq % Q_BLOCK == 0 and t_new >= CONV_W - 1
    assert cache_k.shape[3:] == (N_HEADS, HEAD_DIM)
    ck = jnp.transpose(cache_k, (0, 1, 3, 4, 2))
    cv = jnp.transpose(cache_v, (0, 1, 3, 4, 2))

    ssm = (ssm_a_re, ssm_a_im, ssm_log_dt, ssm_b_re, ssm_b_im, ssm_c_re, ssm_c_im, ssm_d, w_glu, b_glu)
    lng, lnb = ln_in_g.reshape(1, -1), ln_in_b.reshape(1, -1)

    tm_p = _largest_divisor(T, PROMPT_ROWS_CAP, BF16_ROWS)
    tc_p = _largest_divisor(T, SCAN_STEPS_CAP, SUBLANES)
    rows_s = S * t_new
    tm_s = _largest_divisor(rows_s, SAMPLE_ROWS_CAP, BF16_ROWS * t_new)

    meta = jnp.broadcast_to(meta_tokens[None], (B, N_META, D_MODEL))
    xp = jnp.concatenate([meta, x_prompt], axis=1)
    xs = x_sample.reshape(1, rows_s, D_MODEL)
    zeros_state = jnp.zeros((B, N_STATE), F32)
    zeros_conv = jnp.zeros((B, 2 * N_FF_CHUNKS, SUBLANES, FF_CHUNK), F32)

    outs_p, outs_s = [], []
    for l in range(depth):
        w = _layer_weights(l, w_in, ssm, w_branch_ssm, w_branch_attn, w_out, ln1_g, ln1_b, w_up, conv_w,
                           conv_b, w_down, ln2_g, ln2_b, sb_bias)
        first = l == 0

        res = _in_proj(xp, lng, lnb, w["w_in"], tm=tm_p, ln_input=first)
        if first:
            xp, *res = res
        u, q, k, v, gs, ga = res
        z, hr, hi = _s5(u, zeros_state, zeros_state, w["s5"], tc=tc_p)
        o = _attn_prompt(q, k, v, w["sb_bias"])
        x1 = _mix(xp, gs, ga, z, o, w["wbs"], w["wba"], w["wout"],
                  w["ln1_g"], w["ln1_b"], tm=tm_p, alpha=alpha)
        xp, tail = _ffn(x1, zeros_conv, w["wup"], w["conv_w"], w["conv_b"], w["wdown"], w["ln2_g"],
                        w["ln2_b"], tm=tm_p, alpha=alpha, seq_len=tm_p, carry_rows=True)
        outs_p.append((k.reshape(B, T, N_HEADS, HEAD_DIM), v.reshape(B, T, N_HEADS, HEAD_DIM),
                       hr.reshape(B, SSM_GROUPS, SSM_STATE), hi.reshape(B, SSM_GROUPS, SSM_STATE),
                       _from_chunk_major(tail)[:, SUBLANES - (CONV_W - 1):]))

        res = _in_proj(xs, lng, lnb, w["w_in"], tm=tm_s, ln_input=first)
        if first:
            xs, *res = res
        u, q, k, v, gs, ga = res
        z, hr, hi = _s5(u.reshape(S, t_new, SSM_WIDTH), state_ssm_re[l].reshape(S, N_STATE),
                        state_ssm_im[l].reshape(S, N_STATE), w["s5"], tc=t_new)
        k3, v3 = k.reshape(S, t_new, ATTN_WIDTH), v.reshape(S, t_new, ATTN_WIDTH)
        bias_rows = jnp.repeat(w["sb_bias"], t_new).reshape(N_HEADS * t_new, 1)
        o = _attn_sample(q.reshape(S, t_new, ATTN_WIDTH), k3, v3, ck, cv, page_table, bias_rows, l)
        x1 = _mix(xs, gs, ga, z.reshape(1, rows_s, SSM_WIDTH), o.reshape(1, rows_s, ATTN_WIDTH), w["wbs"],
                  w["wba"], w["wout"],
                  w["ln1_g"], w["ln1_b"], tm=tm_s, alpha=alpha)
        prev = state_conv[l].reshape(S, CONV_W - 1, 2 * N_FF_CHUNKS, FF_CHUNK).transpose(1, 2, 0, 3)[None]
        xs, up = _ffn(x1, prev, w["wup"], w["conv_w"], w["conv_b"], w["wdown"], w["ln2_g"],
                      w["ln2_b"], tm=tm_s, alpha=alpha, seq_len=t_new, carry_rows=False)
        outs_s.append((k3.reshape(S, t_new, N_HEADS, HEAD_DIM), v3.reshape(S, t_new, N_HEADS, HEAD_DIM),
                       hr.reshape(S, SSM_GROUPS, SSM_STATE), hi.reshape(S, SSM_GROUPS, SSM_STATE),
                       up[0].transpose(2, 0, 1, 3).reshape(S, CONV_W - 1, 2 * D_FF)))

    stack = lambda outs, j: jnp.stack([o[j] for o in outs])
    return (xp[:, N_META:], xs.reshape(S, t_new, D_MODEL),
            *[stack(outs_p, j) for j in range(5)], *[stack(outs_s, j) for j in range(5)])
```
